```python
import jax
import jax.numpy as jnp
from jax import lax
import numpy as np

D_MODEL = 1024
BATCH = 16
SEQ = 2048
DEPTH = 4

GRID_W = 64
CTX_LEN = 256
N_MIXERS = 3
N_ATTN_LAYERS = (DEPTH + 2) // 3
N_CONV_LAYERS = (DEPTH + 1) // 3
N_NAT_LAYERS = DEPTH // 3

A_HEAD_DIM = 64
A_HEADS = D_MODEL // A_HEAD_DIM
A_KV_HEADS = A_HEADS // 4
A_GROUP = A_HEADS // A_KV_HEADS
A_Q = A_HEADS * A_HEAD_DIM
A_KV = A_KV_HEADS * A_HEAD_DIM
A_WINDOW = 128
A_BLOCK = 128
ROPE_BASE = 10000.0
CONV_WIDTH = 3
NA_HEAD_DIM = 64
NA_HEADS = D_MODEL // NA_HEAD_DIM
NA_KH_MAX = 8
NA_KW = 16
N_GROUPS = 4
EXPERTS_PER_GROUP = 8
N_EXPERTS = N_GROUPS * EXPERTS_PER_GROUP
TOP_K = 2
EXPERT_FF = D_MODEL // 2
MOE_BLOCK = 128
ALPHA = (2 * DEPTH) ** 0.25
BETA = (8 * DEPTH) ** -0.25
LN_EPS = 1e-5
NEG_INF = -1e30

kernel_name = 'hybrid_dit_window_conv_nat_hmoe'


def layer_norm(x, g, b):
    xf = x.astype(jnp.float32)
    mu = jnp.mean(xf, axis=-1, keepdims=True)
    var = jnp.mean(jnp.square(xf - mu), axis=-1, keepdims=True)
    return ((xf - mu) * lax.rsqrt(var + LN_EPS) * g + b).astype(x.dtype)


def modulate(x, shift, scale):
    return x * (1.0 + scale) + shift


def axial_rope(x, rows, cols):
    half = x.shape[-1] // 2
    quarter = half // 2
    inv_freq = ROPE_BASE ** (-jnp.arange(quarter, dtype=jnp.float32) / quarter)

    def rotate(xp, pos):
        ang = pos.astype(jnp.float32)[:, None] * inv_freq
        cos = jnp.cos(ang)[None, :, None, :]
        sin = jnp.sin(ang)[None, :, None, :]
        x1 = xp[..., :quarter].astype(jnp.float32)
        x2 = xp[..., quarter:].astype(jnp.float32)
        return jnp.concatenate([x1 * cos - x2 * sin, x1 * sin + x2 * cos], axis=-1)

    out = jnp.concatenate([rotate(x[..., :half], rows), rotate(x[..., half:], cols)], axis=-1)
    return out.astype(x.dtype)


def sink_softmax(s, sink):
    sk = jnp.broadcast_to(sink, s.shape[:-1] + (1,))
    p = jax.nn.softmax(jnp.concatenate([s, sk], axis=-1), axis=-1)
    return p[..., :-1]


def windowed_gqa(h_lat, h_ctx, w_qkv, w_o, sink, with_ctx_out):
    B, S, _ = h_lat.shape
    L = h_ctx.shape[1]
    nb = S // A_BLOCK
    span = 3 * A_BLOCK
    scale = A_HEAD_DIM ** -0.5

    def split_qkv(h):
        qkv = h @ w_qkv
        n = h.shape[1]
        q = qkv[..., :A_Q].reshape(B, n, A_HEADS, A_HEAD_DIM)
        k = qkv[..., A_Q:A_Q + A_KV].reshape(B, n, A_KV_HEADS, A_HEAD_DIM)
        v = qkv[..., A_Q + A_KV:].reshape(B, n, A_KV_HEADS, A_HEAD_DIM)
        return q, k, v

    q, k, v = split_qkv(h_lat)
    qc, kc, vc = split_qkv(h_ctx)
    t = jnp.arange(S)
    q = axial_rope(q, t // GRID_W, t % GRID_W)
    k = axial_rope(k, t // GRID_W, t % GRID_W)
    sink_f = sink.astype(jnp.float32).reshape(A_KV_HEADS, A_GROUP, 1, 1)

    pad = ((0, 0), (A_BLOCK, A_BLOCK), (0, 0), (0, 0))
    kp = jnp.pad(k, pad)
    vp = jnp.pad(v, pad)
    qb = q.reshape(B, nb, A_BLOCK, A_KV_HEADS, A_GROUP, A_HEAD_DIM).transpose(1, 0, 2, 3, 4, 5)
    rel = jnp.arange(span)[None, :] - A_BLOCK - jnp.arange(A_BLOCK)[:, None]
    band = jnp.abs(rel) <= A_WINDOW
    ctx_ok = jnp.ones((A_BLOCK, L), dtype=bool)

    def block(args):
        b, qi = args
        kb = lax.dynamic_slice_in_dim(kp, b * A_BLOCK, span, axis=1)
        vb = lax.dynamic_slice_in_dim(vp, b * A_BLOCK, span, axis=1)
        kpos = b * A_BLOCK - A_BLOCK + jnp.arange(span)
        valid = band & ((kpos >= 0) & (kpos < S))[None, :]
        mask = jnp.concatenate([valid, ctx_ok], axis=1)
        k_all = jnp.concatenate([kb, kc], axis=1)
        v_all = jnp.concatenate([vb, vc], axis=1)
        s = jnp.einsum('bqkgd,bskd->bkgqs', qi, k_all).astype(jnp.float32) * scale
        p = sink_softmax(jnp.where(mask, s, NEG_INF), sink_f)
        return jnp.einsum('bkgqs,bskd->bqkgd', p.astype(v_all.dtype), v_all)

    o = lax.map(block, (jnp.arange(nb), qb))
    o_lat = o.transpose(1, 0, 2, 3, 4, 5).reshape(B, S, A_Q) @ w_o
    if not with_ctx_out:
        return o_lat, None
    qcg = qc.reshape(B, L, A_KV_HEADS, A_GROUP, A_HEAD_DIM)
    s = jnp.einsum('blkgd,bmkd->bkglm', qcg, kc).astype(jnp.float32) * scale
    p = sink_softmax(s, sink_f)
    o_ctx = jnp.einsum('bkglm,bmkd->blkgd', p.astype(vc.dtype), vc).reshape(B, L, A_Q) @ w_o
    return o_lat, o_ctx


def depthwise_conv(u, w):
    return lax.conv_general_dilated(
        u, w[:, None, :].astype(u.dtype), window_strides=(1,),
        padding=[(CONV_WIDTH // 2, CONV_WIDTH // 2)],
        dimension_numbers=('NWC', 'WIO', 'NWC'), feature_group_count=u.shape[-1])


def gated_short_conv(h_lat, h_ctx, w_in, conv_w, w_out, with_ctx_out):
    def run(h):
        gb, gc, u = jnp.split(h @ w_in, 3, axis=-1)
        return (gb * depthwise_conv(gc * u, conv_w)) @ w_out
    return run(h_lat), (run(h_ctx) if with_ctx_out else None)


def neighbourhood_attention(h_lat, h_ctx, w_qkv, w_o, rpb, with_ctx_out):
    B, S, _ = h_lat.shape
    L = h_ctx.shape[1]
    rows = S // GRID_W
    kh = min(NA_KH_MAX, rows)
    ncb = GRID_W // NA_KW
    span = 2 * NA_KW
    scale = NA_HEAD_DIM ** -0.5

    def split_qkv(h):
        q, k, v = jnp.split(h @ w_qkv, 3, axis=-1)
        shp = (B, h.shape[1], NA_HEADS, NA_HEAD_DIM)
        return q.reshape(shp), k.reshape(shp), v.reshape(shp)

    q, k, v = split_qkv(h_lat)
    qc, kc, vc = split_qkv(h_ctx)
    grid = (B, rows, GRID_W, NA_HEADS, NA_HEAD_DIM)
    qg = jnp.moveaxis(q.reshape(grid), 1, 0)
    kg = k.reshape(grid)
    vg = v.reshape(grid)

    cb = jnp.arange(ncb)
    key_cols = jnp.clip(cb * NA_KW - NA_KW // 2, 0, GRID_W - span)[:, None] + jnp.arange(span)
    q_cols = cb[:, None] * NA_KW + jnp.arange(NA_KW)
    q_start = jnp.clip(q_cols - NA_KW // 2, 0, GRID_W - NA_KW)
    col_ok = (key_cols[:, None, :] >= q_start[:, :, None]) & (key_cols[:, None, :] < q_start[:, :, None] + NA_KW)
    loc_mask = jnp.broadcast_to(col_ok[:, None, :, None, :], (ncb, 1, NA_KW, kh, span)).reshape(ncb, 1, NA_KW, kh * span)
    dc_idx = jnp.clip(key_cols[:, None, :] - q_cols[:, :, None] + NA_KW - 1, 0, 2 * NA_KW - 2)
    rpb_f = rpb.astype(jnp.float32)

    def row_step(args):
        r, q_row = args
        rs = jnp.clip(r - kh // 2, 0, rows - kh)

        def gather(t):
            t_rows = lax.dynamic_slice_in_dim(t, rs, kh, axis=1)
            t_blk = t_rows[:, :, key_cols]
            return t_blk.transpose(0, 2, 1, 3, 4, 5).reshape(B, ncb, kh * span, NA_HEADS, NA_HEAD_DIM)

        kb = gather(kg)
        vb = gather(vg)
        qb = q_row.reshape(B, ncb, NA_KW, NA_HEADS, NA_HEAD_DIM)
        dr_idx = rs + jnp.arange(kh) - r + NA_KH_MAX - 1
        bias = rpb_f[:, dr_idx][:, :, dc_idx]
        bias = bias.transpose(2, 0, 3, 1, 4).reshape(ncb, NA_HEADS, NA_KW, kh * span)
        s_loc = jnp.einsum('bnqhd,bnkhd->bnhqk', qb, kb).astype(jnp.float32) * scale + bias
        s_ctx = jnp.einsum('bnqhd,bmhd->bnhqm', qb, kc).astype(jnp.float32) * scale
        s = jnp.concatenate([jnp.where(loc_mask, s_loc, NEG_INF), s_ctx], axis=-1)
        p = jax.nn.softmax(s, axis=-1).astype(vb.dtype)
        o = (jnp.einsum('bnhqk,bnkhd->bnqhd', p[..., :kh * span], vb)
             + jnp.einsum('bnhqm,bmhd->bnqhd', p[..., kh * span:], vc))
        return o.reshape(B, GRID_W, NA_HEADS * NA_HEAD_DIM)

    o = lax.map(row_step, (jnp.arange(rows), qg))
    o_lat = jnp.moveaxis(o, 0, 1).reshape(B, S, D_MODEL) @ w_o
    if not with_ctx_out:
        return o_lat, None
    s = jnp.einsum('blhd,bmhd->bhlm', qc, kc).astype(jnp.float32) * scale
    p = jax.nn.softmax(s, axis=-1).astype(vc.dtype)
    o_ctx = jnp.einsum('bhlm,bmhd->blhd', p, vc).reshape(B, L, D_MODEL) @ w_o
    return o_lat, o_ctx


def hier_moe(h, w_group, b_group, w_expert, b_expert, w_gate, w_up, w_down):
    T, D = h.shape
    g_logits = (h @ w_group).astype(jnp.float32) + b_group.astype(jnp.float32)
    g_idx = jnp.argmax(g_logits, axis=-1)
    g_prob = jnp.take_along_axis(jax.nn.softmax(g_logits, axis=-1), g_idx[:, None], axis=-1)
    e_logits = ((h @ w_expert).astype(jnp.float32) + b_expert.astype(jnp.float32)).reshape(T, N_GROUPS, EXPERTS_PER_GROUP)
    e_logits = jnp.take_along_axis(e_logits, g_idx[:, None, None], axis=1)[:, 0]
    top_v, top_i = lax.top_k(e_logits, TOP_K)
    gates = (jax.nn.softmax(top_v, axis=-1) * g_prob).reshape(-1)
    eid = (g_idx[:, None] * EXPERTS_PER_GROUP + top_i).reshape(-1)
    tok = jnp.repeat(jnp.arange(T), TOP_K)
    n_assign = T * TOP_K

    order = jnp.argsort(eid)
    s_eid, s_tok, s_gate = eid[order], tok[order], gates[order]
    counts = jnp.bincount(eid, length=N_EXPERTS)
    pcounts = (counts + MOE_BLOCK - 1) // MOE_BLOCK * MOE_BLOCK
    offs = jnp.cumsum(counts) - counts
    pend = jnp.cumsum(pcounts)
    poffs = pend - pcounts
    dest = poffs[s_eid] + jnp.arange(n_assign) - offs[s_eid]
    n_blocks = -(-n_assign // MOE_BLOCK) + N_EXPERTS
    xs = jnp.zeros((n_blocks * MOE_BLOCK, D), h.dtype).at[dest].set(h[s_tok])
    blk_e = jnp.minimum(jnp.searchsorted(pend, jnp.arange(n_blocks) * MOE_BLOCK, side='right'), N_EXPERTS - 1)

    def expert_block(args):
        xb, e = args
        return (jax.nn.silu(xb @ w_gate[e]) * (xb @ w_up[e])) @ w_down[e]

    ys = lax.map(expert_block, (xs.reshape(n_blocks, MOE_BLOCK, D), blk_e)).reshape(n_blocks * MOE_BLOCK, D)
    return jax.ops.segment_sum(ys[dest] * s_gate[:, None].astype(ys.dtype), s_tok, num_segments=T)


def setup_inputs(seed: int = 0) -> dict:
    key = jax.random.key(seed)
    ks = iter(jax.random.split(key, 32))

    def nrm(shape, scale):
        return scale * jax.random.normal(next(ks), shape, jnp.float32)

    D = D_MODEL
    inv = D ** -0.5
    a_col = jnp.concatenate([jnp.ones((A_Q + A_KV,), jnp.float32), jnp.full((A_KV,), BETA, jnp.float32)])
    n_col = jnp.concatenate([jnp.ones((2 * D,), jnp.float32), jnp.full((D,), BETA, jnp.float32)])
    return {
        'x': nrm((BATCH, SEQ, D), 1.0),
        'c': nrm((BATCH, D), 1.0),
        'ctx': nrm((BATCH, CTX_LEN, D), 1.0),
        'c_ctx': nrm((D,), 1.0),
        'w_ada': nrm((DEPTH, D, 6 * D), 0.5 * inv),
        'b_ada': nrm((DEPTH, 6 * D), 0.02),
        'ln1_g': 1.0 + nrm((DEPTH, D), 0.02),
        'ln1_b': nrm((DEPTH, D), 0.02),
        'ln2_g': 1.0 + nrm((DEPTH, D), 0.02),
        'ln2_b': nrm((DEPTH, D), 0.02),
        'attn_w_qkv': nrm((N_ATTN_LAYERS, D, A_Q + 2 * A_KV), inv) * a_col,
        'attn_w_o': nrm((N_ATTN_LAYERS, A_Q, D), BETA * A_Q ** -0.5),
        'attn_sink': nrm((N_ATTN_LAYERS, A_HEADS), 0.5),
        'conv_w_in': nrm((N_CONV_LAYERS, D, 3 * D), inv),
        'conv_w': nrm((N_CONV_LAYERS, CONV_WIDTH, D), CONV_WIDTH ** -0.5),
        'conv_w_out': nrm((N_CONV_LAYERS, D, D), BETA * inv),
        'nat_w_qkv': nrm((N_NAT_LAYERS, D, 3 * D), inv) * n_col,
        'nat_w_o': nrm((N_NAT_LAYERS, D, D), BETA * inv),
        'nat_rpb': nrm((N_NAT_LAYERS, NA_HEADS, 2 * NA_KH_MAX - 1, 2 * NA_KW - 1), 0.1),
        'router_w_group': nrm((DEPTH, D, N_GROUPS), inv),
        'router_b_group': nrm((DEPTH, N_GROUPS), 0.01),
        'router_w_expert': nrm((DEPTH, D, N_EXPERTS), inv),
        'router_b_expert': nrm((DEPTH, N_EXPERTS), 0.01),
        'expert_w_gate': nrm((DEPTH, N_EXPERTS, D, EXPERT_FF), inv),
        'expert_w_up': nrm((DEPTH, N_EXPERTS, D, EXPERT_FF), inv),
        'expert_w_down': nrm((DEPTH, N_EXPERTS, EXPERT_FF, D), BETA * EXPERT_FF ** -0.5),
    }


def reference(x, c, ctx, c_ctx, w_ada, b_ada, ln1_g, ln1_b, ln2_g, ln2_b,
              attn_w_qkv, attn_w_o, attn_sink, conv_w_in, conv_w, conv_w_out,
              nat_w_qkv, nat_w_o, nat_rpb,
              router_w_group, router_b_group, router_w_expert, router_b_expert,
              expert_w_gate, expert_w_up, expert_w_down):
    B, S, D = x.shape
    xc = ctx
    silu_c = jax.nn.silu(c)
    silu_cc = jax.nn.silu(c_ctx)
    for i in range(DEPTH):
        last = i == DEPTH - 1
        j = i // N_MIXERS
        kind = i % N_MIXERS
        mod_lat = jnp.split((silu_c @ w_ada[i] + b_ada[i])[:, None, :], 6, axis=-1)
        mod_ctx = jnp.split(silu_cc @ w_ada[i] + b_ada[i], 6, axis=-1)

        h_lat = modulate(x, mod_lat[0], mod_lat[1])
        h_ctx = modulate(xc, mod_ctx[0], mod_ctx[1])
        if kind == 0:
            o_lat, o_ctx = windowed_gqa(h_lat, h_ctx, attn_w_qkv[j], attn_w_o[j], attn_sink[j], not last)
        elif kind == 1:
            o_lat, o_ctx = gated_short_conv(h_lat, h_ctx, conv_w_in[j], conv_w[j], conv_w_out[j], not last)
        else:
            o_lat, o_ctx = neighbourhood_attention(h_lat, h_ctx, nat_w_qkv[j], nat_w_o[j], nat_rpb[j], not last)
        x = layer_norm(ALPHA * x + mod_lat[2] * o_lat, ln1_g[i], ln1_b[i])

        h_lat = modulate(x, mod_lat[3], mod_lat[4]).reshape(B * S, D)
        moe_args = (router_w_group[i], router_b_group[i], router_w_expert[i], router_b_expert[i],
                    expert_w_gate[i], expert_w_up[i], expert_w_down[i])
        if last:
            y_lat = hier_moe(h_lat, *moe_args)
        else:
            xc = layer_norm(ALPHA * xc + mod_ctx[2] * o_ctx, ln1_g[i], ln1_b[i])
            h_ctx = modulate(xc, mod_ctx[3], mod_ctx[4]).reshape(-1, D)
            y = hier_moe(jnp.concatenate([h_lat, h_ctx], axis=0), *moe_args)
            y_lat = y[:B * S]
            xc = layer_norm(ALPHA * xc + mod_ctx[5] * y[B * S:].reshape(xc.shape), ln2_g[i], ln2_b[i])
        x = layer_norm(ALPHA * x + mod_lat[5] * y_lat.reshape(B, S, D), ln2_g[i], ln2_b[i])
    return x
```

```python
import functools

import numpy as np
import jax
import jax.numpy as jnp
from jax import lax
from jax.experimental import pallas as pl
from jax.experimental.pallas import tpu as pltpu

F32 = jnp.float32
BF16 = jnp.bfloat16
HIGHEST = lax.Precision.HIGHEST

D_MODEL = 1024
HEAD_DIM = 64
N_HEADS = D_MODEL // HEAD_DIM
A_KV_HEADS = N_HEADS // 4
A_KV = A_KV_HEADS * HEAD_DIM
A_WINDOW = 128
A_BLOCK = 128
GRID_W = 64
ROPE_BASE = 10000.0
NA_KH_MAX = 8
NA_KW = 16
N_GROUPS = 4
EXPERTS_PER_GROUP = 8
N_EXPERTS = N_GROUPS * EXPERTS_PER_GROUP
TOP_K = 2
EXPERT_FF = D_MODEL // 2
LN_EPS = 1e-5
NEG_INF = -1e30

LANES = 128
ADA_ROWS = 24
ROUTE_W = 8
TOKEN_TILE = 512
COMBINE_TILE = 256
EXPERT_BLOCK = 256
VMEM_LIMIT = 56 * 1024 * 1024


def _cparams(n_axes=1):
    return pltpu.CompilerParams(dimension_semantics=("arbitrary",) * n_axes, vmem_limit_bytes=VMEM_LIMIT)


def _ada_kernel(c_ref, w_ref, b_ref, o_ref):
    cc = c_ref[...]
    s = cc / (1.0 + jnp.exp(-cc))
    o_ref[...] = jnp.dot(s, w_ref[...], precision=HIGHEST, preferred_element_type=F32) + b_ref[...]


def _ada(cc, w_ada, b_ada):
    depth, d, n = w_ada.shape
    nt = n // 4
    return pl.pallas_call(
        _ada_kernel,
        out_shape=jax.ShapeDtypeStruct((depth, ADA_ROWS, n), F32),
        grid=(depth, n // nt),
        in_specs=[
            pl.BlockSpec((ADA_ROWS, d), lambda i, j: (0, 0)),
            pl.BlockSpec((None, d, nt), lambda i, j: (i, 0, j)),
            pl.BlockSpec((None, 1, nt), lambda i, j: (i, 0, j)),
        ],
        out_specs=pl.BlockSpec((None, ADA_ROWS, nt), lambda i, j: (i, 0, j)),
        compiler_params=_cparams(2),
        name="ada_mod",
    )(cc, w_ada, b_ada.reshape(depth, 1, n))


def _modulated(x_ref, mod_ref, shift_row):
    return x_ref[...] * (1.0 + mod_ref[shift_row + 1:shift_row + 2, :]) + mod_ref[shift_row:shift_row + 1, :]


def _layer_norm(r, g_ref, b_ref):
    mu = jnp.mean(r, axis=-1, keepdims=True)
    rc = r - mu
    var = jnp.mean(rc * rc, axis=-1, keepdims=True)
    return rc * lax.rsqrt(var + LN_EPS) * g_ref[...] + b_ref[...]


def _dot_nt(a, b):
    return lax.dot_general(a, b, (((1,), (1,)), ((), ())), preferred_element_type=F32)


def _softmax_pv(s, v, sink_col=None):
    m = jnp.max(s, axis=-1, keepdims=True)
    if sink_col is not None:
        m = jnp.maximum(m, sink_col)
    e = jnp.exp(s - m)
    den = jnp.sum(e, axis=-1, keepdims=True)
    if sink_col is not None:
        den = den + jnp.exp(sink_col - m)
    return jnp.dot(e.astype(BF16), v, preferred_element_type=F32) / den


def _seg_of_tile(i, n_lat_tiles, tiles_per_batch, n_batch):
    return jnp.where(i < n_lat_tiles, i // tiles_per_batch, n_batch)


def _proj_attn_kernel(x_ref, mod_ref, cos_ref, sin_ref, w_ref, q_ref, k_ref, v_ref):
    h = _modulated(x_ref, mod_ref, 0).astype(BF16)
    cos = cos_ref[...]
    sin = sin_ref[...]
    lane = lax.broadcasted_iota(jnp.int32, cos.shape, 1)
    low_half = (lane % 32) < 16

    def rope(a):
        partner = jnp.where(low_half, pltpu.roll(a, LANES - 16, 1), pltpu.roll(a, 16, 1))
        return a * cos + partner * sin

    nq = q_ref.shape[1]
    nk = k_ref.shape[1]
    for j in range(nq // LANES):
        a = jnp.dot(h, w_ref[:, j * LANES:(j + 1) * LANES], preferred_element_type=F32)
        q_ref[:, j * LANES:(j + 1) * LANES] = (rope(a) * (HEAD_DIM ** -0.5)).astype(BF16)
    for j in range(nk // LANES):
        a = jnp.dot(h, w_ref[:, nq + j * LANES:nq + (j + 1) * LANES], preferred_element_type=F32)
        k_ref[:, j * LANES:(j + 1) * LANES] = rope(a).astype(BF16)
    v_ref[...] = jnp.dot(h, w_ref[:, nq + nk:], preferred_element_type=F32).astype(BF16)


def _proj_nat_kernel(x_ref, mod_ref, w_ref, q_ref, k_ref, v_ref):
    h = _modulated(x_ref, mod_ref, 0).astype(BF16)
    d = q_ref.shape[1]
    q_ref[...] = (jnp.dot(h, w_ref[:, :d], preferred_element_type=F32) * (HEAD_DIM ** -0.5)).astype(BF16)
    k_ref[...] = jnp.dot(h, w_ref[:, d:2 * d], preferred_element_type=F32).astype(BF16)
    v_ref[...] = jnp.dot(h, w_ref[:, 2 * d:], preferred_element_type=F32).astype(BF16)


def _proj_conv_kernel(x_ref, mod_ref, w_ref, gb_ref, z_ref):
    h = _modulated(x_ref, mod_ref, 0).astype(BF16)
    d = gb_ref.shape[1]
    gb_ref[...] = jnp.dot(h, w_ref[:, :d], preferred_element_type=F32).astype(BF16)
    gc = jnp.dot(h, w_ref[:, d:2 * d], preferred_element_type=F32)
    u = jnp.dot(h, w_ref[:, 2 * d:], preferred_element_type=F32)
    z_ref[...] = (gc * u).astype(BF16)


def _proj_in(kind, xs, mods, w_bf, dims, rope_tabs=None):
    n_batch, seq, ctx_len = dims
    t, d = xs.shape
    tm = TOKEN_TILE
    n_lat_tiles = n_batch * seq // tm
    tpb = seq // tm
    seg = functools.partial(_seg_of_tile, n_lat_tiles=n_lat_tiles, tiles_per_batch=tpb, n_batch=n_batch)
    x_spec = pl.BlockSpec((tm, d), lambda i: (i, 0))
    mod_spec = pl.BlockSpec((None, 6, d), lambda i: (seg(i), 0, 0))
    w_spec = pl.BlockSpec(w_bf.shape, lambda i: (0, 0))

    def tok_out(n):
        return jax.ShapeDtypeStruct((t, n), BF16), pl.BlockSpec((tm, n), lambda i: (i, 0))

    if kind == 0:
        cos, sin = rope_tabs
        tab_spec = pl.BlockSpec((tm, LANES), lambda i: (jnp.where(i < n_lat_tiles, i % tpb, tpb), 0))
        outs = [tok_out(d), tok_out(A_KV), tok_out(A_KV)]
        body, ins, in_specs = _proj_attn_kernel, (xs, mods, cos, sin, w_bf), [x_spec, mod_spec, tab_spec, tab_spec, w_spec]
    elif kind == 1:
        outs = [tok_out(d), tok_out(d)]
        body, ins, in_specs = _proj_conv_kernel, (xs, mods, w_bf), [x_spec, mod_spec, w_spec]
    else:
        outs = [tok_out(d), tok_out(d), tok_out(d)]
        body, ins, in_specs = _proj_nat_kernel, (xs, mods, w_bf), [x_spec, mod_spec, w_spec]
    return pl.pallas_call(
        body,
        out_shape=[o[0] for o in outs],
        grid=(t // tm,),
        in_specs=in_specs,
        out_specs=[o[1] for o in outs],
        compiler_params=_cparams(1),
        name=f"proj_in_{kind}",
    )(*ins)


def _rope_tables(seq, tm):
    quarter = HEAD_DIM // 4
    inv_freq = ROPE_BASE ** (-jnp.arange(quarter, dtype=F32) / quarter)
    tpos = jnp.arange(seq)
    ang_r = (tpos // GRID_W).astype(F32)[:, None] * inv_freq
    ang_c = (tpos % GRID_W).astype(F32)[:, None] * inv_freq
    cos_h = jnp.concatenate([jnp.cos(ang_r), jnp.cos(ang_r), jnp.cos(ang_c), jnp.cos(ang_c)], axis=1)
    sin_h = jnp.concatenate([-jnp.sin(ang_r), jnp.sin(ang_r), -jnp.sin(ang_c), jnp.sin(ang_c)], axis=1)
    reps = LANES // HEAD_DIM
    cos = jnp.concatenate([jnp.tile(cos_h, (1, reps)), jnp.ones((tm, LANES), F32)], axis=0)
    sin = jnp.concatenate([jnp.tile(sin_h, (1, reps)), jnp.zeros((tm, LANES), F32)], axis=0)
    return cos, sin


def _window_attn_kernel(sink_ref, q_ref, kp_ref, kc_ref, kn_ref, kx_ref, vp_ref, vc_ref, vn_ref, vx_ref, o_ref, *, seq):
    qi = pl.program_id(1)
    blk = A_BLOCK
    group = N_HEADS // A_KV_HEADS
    kcat = jnp.concatenate([kp_ref[...], kc_ref[...], kn_ref[...], kx_ref[...]], axis=0)
    vcat = jnp.concatenate([vp_ref[...], vc_ref[...], vn_ref[...], vx_ref[...]], axis=0)
    nkeys = kcat.shape[0]
    rows = group * blk
    row = lax.broadcasted_iota(jnp.int32, (rows, nkeys), 0) & (blk - 1)
    col = lax.broadcasted_iota(jnp.int32, (rows, nkeys), 1)
    kpos = qi * blk - blk + col
    rel = col - blk - row
    local_ok = (rel >= -A_WINDOW) & (rel <= A_WINDOW) & (kpos >= 0) & (kpos < seq)
    valid = local_ok | (col >= 3 * blk)
    for kk in range(A_KV_HEADS):
        k_h = kcat[:, kk * HEAD_DIM:(kk + 1) * HEAD_DIM]
        v_h = vcat[:, kk * HEAD_DIM:(kk + 1) * HEAD_DIM]
        heads = [kk * group + g for g in range(group)]
        q_st = jnp.concatenate([q_ref[:, h * HEAD_DIM:(h + 1) * HEAD_DIM] for h in heads], axis=0)
        sink_col = jnp.concatenate([jnp.full((blk, 1), sink_ref[h], F32) for h in heads], axis=0)
        s = jnp.where(valid, _dot_nt(q_st, k_h), NEG_INF)
        o = _softmax_pv(s, v_h, sink_col).astype(o_ref.dtype)
        for g, h in enumerate(heads):
            o_ref[:, h * HEAD_DIM:(h + 1) * HEAD_DIM] = o[g * blk:(g + 1) * blk, :]


def _window_attn(q, k, v, sink, dims):
    n_batch, seq, ctx_len = dims
    t, d = q.shape
    blk = A_BLOCK
    nb = seq // blk
    t_lat = n_batch * seq
    ctx_blk0 = t_lat // ctx_len

    def q_map(b, i, s):
        return (b * nb + i, 0)

    def kv_map(off):
        return lambda b, i, s: (b * nb + jnp.clip(i + off, 0, nb - 1), 0)

    def ctx_map(b, i, s):
        return (ctx_blk0 + b, 0)

    kv_specs = [pl.BlockSpec((blk, A_KV), kv_map(-1)), pl.BlockSpec((blk, A_KV), kv_map(0)),
                pl.BlockSpec((blk, A_KV), kv_map(1)), pl.BlockSpec((ctx_len, A_KV), ctx_map)]
    return pl.pallas_call(
        functools.partial(_window_attn_kernel, seq=seq),
        out_shape=jax.ShapeDtypeStruct((t_lat, d), BF16),
        grid_spec=pltpu.PrefetchScalarGridSpec(
            num_scalar_prefetch=1,
            grid=(n_batch, nb),
            in_specs=[pl.BlockSpec((blk, d), q_map)] + kv_specs + kv_specs,
            out_specs=pl.BlockSpec((blk, d), q_map),
        ),
        compiler_params=_cparams(2),
        name="window_attn",
    )(sink, q, k, k, k, k, v, v, v, v)


def _ctx_attn_kernel(sink_ref, q_ref, k_ref, v_ref, o_ref, *, n_kv, use_sink):
    group = N_HEADS // n_kv
    n = q_ref.shape[0]
    for kk in range(n_kv):
        k_h = k_ref[:, kk * HEAD_DIM:(kk + 1) * HEAD_DIM]
        v_h = v_ref[:, kk * HEAD_DIM:(kk + 1) * HEAD_DIM]
        heads = [kk * group + g for g in range(group)]
        q_st = jnp.concatenate([q_ref[:, h * HEAD_DIM:(h + 1) * HEAD_DIM] for h in heads], axis=0)
        sink_col = None
        if use_sink:
            sink_col = jnp.concatenate([jnp.full((n, 1), sink_ref[h], F32) for h in heads], axis=0)
        o = _softmax_pv(_dot_nt(q_st, k_h), v_h, sink_col).astype(o_ref.dtype)
        for g, h in enumerate(heads):
            o_ref[:, h * HEAD_DIM:(h + 1) * HEAD_DIM] = o[g * n:(g + 1) * n, :]


def _ctx_attn(q, k, v, sink, dims, n_kv, use_sink):
    n_batch, seq, ctx_len = dims
    d = q.shape[1]
    kvw = n_kv * HEAD_DIM
    blk0 = n_batch * seq // ctx_len
    return pl.pallas_call(
        functools.partial(_ctx_attn_kernel, n_kv=n_kv, use_sink=use_sink),
        out_shape=jax.ShapeDtypeStruct((n_batch * ctx_len, d), BF16),
        grid_spec=pltpu.PrefetchScalarGridSpec(
            num_scalar_prefetch=1,
            grid=(n_batch,),
            in_specs=[pl.BlockSpec((ctx_len, d), lambda b, s: (blk0 + b, 0)),
                      pl.BlockSpec((ctx_len, kvw), lambda b, s: (blk0 + b, 0)),
                      pl.BlockSpec((ctx_len, kvw), lambda b, s: (blk0 + b, 0))],
            out_specs=pl.BlockSpec((ctx_len, d), lambda b, s: (b, 0)),
        ),
        compiler_params=_cparams(1),
        name="ctx_attn",
    )(sink, q, k, v)


def _nat_kernel(q_ref, k_ref, v_ref, kx_ref, vx_ref, bias_ref, o_ref, *, kh, n_rows):
    r = pl.program_id(1)
    rs = jnp.clip(r - kh // 2, 0, n_rows - kh)
    start = pl.multiple_of(rs * GRID_W, GRID_W)
    n_loc = kh * GRID_W
    for h in range(N_HEADS):
        sl = slice(h * HEAD_DIM, (h + 1) * HEAD_DIM)
        q_h = q_ref[:, sl]
        k_loc = k_ref[pl.ds(start, n_loc), sl]
        v_loc = v_ref[pl.ds(start, n_loc), sl]
        s_loc = _dot_nt(q_h, k_loc) + bias_ref[h]
        s_ctx = _dot_nt(q_h, kx_ref[:, sl])
        s = jnp.concatenate([s_loc, s_ctx], axis=-1)
        v_all = jnp.concatenate([v_loc, vx_ref[:, sl]], axis=0)
        o_ref[:, sl] = _softmax_pv(s, v_all).astype(o_ref.dtype)


def _nat_bias_table(rpb, kh):
    qc = jnp.arange(GRID_W)
    kc = jnp.arange(GRID_W)
    q_start = jnp.clip(qc - NA_KW // 2, 0, GRID_W - NA_KW)
    col_ok = (kc[None, :] >= q_start[:, None]) & (kc[None, :] < q_start[:, None] + NA_KW)
    dc = jnp.clip(kc[None, :] - qc[:, None] + NA_KW - 1, 0, 2 * NA_KW - 2)
    case = jnp.arange(kh)
    j = jnp.arange(kh)
    dr = j[None, :] - case[:, None] + NA_KH_MAX - 1
    b = rpb.astype(F32)[:, dr][:, :, :, dc]
    b = jnp.where(col_ok[None, None, None], b, NEG_INF)
    b = b.transpose(1, 0, 3, 2, 4)
    return b.reshape(kh, N_HEADS, GRID_W, kh * GRID_W)


def _nat_attn(q, k, v, rpb, dims):
    n_batch, seq, ctx_len = dims
    t, d = q.shape
    n_rows = seq // GRID_W
    kh = min(NA_KH_MAX, n_rows)
    bias = _nat_bias_table(rpb, kh)
    ctx_blk0 = n_batch * seq // ctx_len

    def case_of(r):
        return r - jnp.clip(r - kh // 2, 0, n_rows - kh)

    return pl.pallas_call(
        functools.partial(_nat_kernel, kh=kh, n_rows=n_rows),
        out_shape=jax.ShapeDtypeStruct((n_batch * seq, d), BF16),
        grid=(n_batch, n_rows),
        in_specs=[
            pl.BlockSpec((GRID_W, d), lambda b, r: (b * n_rows + r, 0)),
            pl.BlockSpec((seq, d), lambda b, r: (b, 0)),
            pl.BlockSpec((seq, d), lambda b, r: (b, 0)),
            pl.BlockSpec((ctx_len, d), lambda b, r: (ctx_blk0 + b, 0)),
            pl.BlockSpec((ctx_len, d), lambda b, r: (ctx_blk0 + b, 0)),
            pl.BlockSpec((None, N_HEADS, GRID_W, kh * GRID_W), lambda b, r: (case_of(r), 0, 0, 0)),
        ],
        out_specs=pl.BlockSpec((GRID_W, d), lambda b, r: (b * n_rows + r, 0)),
        compiler_params=_cparams(2),
        name="nat_attn",
    )(q, k, v, k, v, bias)


def _route(h2, wr_ref, br_ref):
    logits = jnp.dot(h2, wr_ref[...], precision=HIGHEST, preferred_element_type=F32) + br_ref[...]
    lane_i = lax.broadcasted_iota(jnp.int32, logits.shape, 1)
    lane = lane_i.astype(F32)
    big = jnp.float32(1e9)
    low = jnp.float32(-3e38)
    is_group = lane_i < N_GROUPS
    gl = jnp.where(is_group, logits, low)
    gmax = jnp.max(gl, axis=-1, keepdims=True)
    g_idx = jnp.min(jnp.where(gl == gmax, lane, big), axis=-1, keepdims=True)
    gsum = jnp.sum(jnp.where(is_group, jnp.exp(gl - gmax), 0.0), axis=-1, keepdims=True)
    g_prob = 1.0 / gsum
    e_lo = N_GROUPS + g_idx * EXPERTS_PER_GROUP
    in_group = (lane >= e_lo) & (lane < e_lo + EXPERTS_PER_GROUP)
    el = jnp.where(in_group, logits, low)
    v1 = jnp.max(el, axis=-1, keepdims=True)
    i1 = jnp.min(jnp.where(el == v1, lane, big), axis=-1, keepdims=True)
    el2 = jnp.where(lane == i1, low, el)
    v2 = jnp.max(el2, axis=-1, keepdims=True)
    i2 = jnp.min(jnp.where(el2 == v2, lane, big), axis=-1, keepdims=True)
    t = jnp.exp(v2 - v1)
    p1 = 1.0 / (1.0 + t)
    p2 = t / (1.0 + t)
    rec = jnp.where(lane_i == 0, i1 - N_GROUPS,
                    jnp.where(lane_i == 1, i2 - N_GROUPS,
                              jnp.where(lane_i == 2, p1 * g_prob,
                                        jnp.where(lane_i == 3, p2 * g_prob, 0.0))))
    return rec[:, :ROUTE_W]


def _post_mixer(o_bf, x_ref, mod_ref, g_ref, b_ref, wo_ref, wr_ref, br_ref, x1_ref, h2_ref, route_ref, alpha):
    y = jnp.dot(o_bf, wo_ref[...], preferred_element_type=F32)
    x1 = _layer_norm(alpha * x_ref[...] + mod_ref[2:3, :] * y, g_ref, b_ref)
    x1_ref[...] = x1
    h2 = x1 * (1.0 + mod_ref[4:5, :]) + mod_ref[3:4, :]
    h2_ref[...] = h2
    route_ref[...] = _route(h2, wr_ref, br_ref)


def _post_attn_kernel(o_ref, x_ref, mod_ref, g_ref, b_ref, wo_ref, wr_ref, br_ref, x1_ref, h2_ref, route_ref, *, alpha):
    _post_mixer(o_ref[...], x_ref, mod_ref, g_ref, b_ref, wo_ref, wr_ref, br_ref, x1_ref, h2_ref, route_ref, alpha)


def _post_conv_kernel(gb_ref, z_ref, zp_ref, zn_ref, cw_ref, x_ref, mod_ref, g_ref, b_ref, wo_ref, wr_ref, br_ref,
                      x1_ref, h2_ref, route_ref, *, alpha, t_lat, seq, ctx_len):
    i = pl.program_id(0)
    tm, d = z_ref.shape
    halo = zp_ref.shape[0]
    z = z_ref[...].astype(F32)
    row = lax.broadcasted_iota(jnp.int32, (tm, 1), 0)
    g = i * tm + row
    pos = jnp.where(g < t_lat, g % seq, g % ctx_len)
    n_seq = jnp.where(g < t_lat, seq, ctx_len)
    prev_row = zp_ref[halo - 1:halo, :].astype(F32)
    next_row = zn_ref[0:1, :].astype(F32)
    z_prev = jnp.where(row == 0, prev_row, pltpu.roll(z, 1, 0))
    z_next = jnp.where(row == tm - 1, next_row, pltpu.roll(z, tm - 1, 0))
    z_prev = jnp.where(pos == 0, 0.0, z_prev)
    z_next = jnp.where(pos == n_seq - 1, 0.0, z_next)
    y = cw_ref[0:1, :] * z_prev + cw_ref[1:2, :] * z + cw_ref[2:3, :] * z_next
    o = (gb_ref[...].astype(F32) * y).astype(BF16)
    _post_mixer(o, x_ref, mod_ref, g_ref, b_ref, wo_ref, wr_ref, br_ref, x1_ref, h2_ref, route_ref, alpha)


def _post_mixer_call(kind, mixer_out, xs, mods, ln_g, ln_b, wo_bf, wr, br, dims, n_tok, alpha, conv_w=None):
    n_batch, seq, ctx_len = dims
    d = xs.shape[1]
    tm = TOKEN_TILE
    n_tiles = n_tok // tm
    n_lat_tiles = n_batch * seq // tm
    tpb = seq // tm
    seg = functools.partial(_seg_of_tile, n_lat_tiles=n_lat_tiles, tiles_per_batch=tpb, n_batch=n_batch)
    tok = pl.BlockSpec((tm, d), lambda i: (i, 0))
    full = lambda a: pl.BlockSpec(a.shape, lambda i: (0,) * a.ndim)
    common_ins = (xs, mods, ln_g, ln_b, wo_bf, wr, br)
    common_specs = [tok, pl.BlockSpec((None, 6, d), lambda i: (seg(i), 0, 0)), full(ln_g), full(ln_b),
                    full(wo_bf), full(wr), full(br)]
    if kind == 1:
        gb, z = mixer_out
        halo = 16
        hb = tm // halo
        last_blk = z.shape[0] // halo - 1
        ins = (gb, z, z, z, conv_w) + common_ins
        in_specs = [tok, tok,
                    pl.BlockSpec((halo, d), lambda i: (jnp.maximum(i * hb - 1, 0), 0)),
                    pl.BlockSpec((halo, d), lambda i: (jnp.minimum((i + 1) * hb, last_blk), 0)),
                    full(conv_w)] + common_specs
        body = functools.partial(_post_conv_kernel, alpha=alpha, t_lat=n_batch * seq, seq=seq, ctx_len=ctx_len)
    else:
        ins = (mixer_out,) + common_ins
        in_specs = [tok] + common_specs
        body = functools.partial(_post_attn_kernel, alpha=alpha)
    return pl.pallas_call(
        body,
        out_shape=[jax.ShapeDtypeStruct((n_tok, d), F32), jax.ShapeDtypeStruct((n_tok, d), F32),
                   jax.ShapeDtypeStruct((n_tok, ROUTE_W), F32)],
        grid=(n_tiles,),
        in_specs=in_specs,
        out_specs=[tok, tok, pl.BlockSpec((tm, ROUTE_W), lambda i: (i, 0))],
        compiler_params=_cparams(1),
        name=f"post_mixer_{kind}",
    )(*ins)


def _dispatch_plan(route, bm):
    n_tok = route.shape[0]
    eid = route[:, :TOP_K].astype(jnp.int32).reshape(-1)
    n_assign = n_tok * TOP_K
    onehot = (eid[:, None] == jnp.arange(N_EXPERTS)[None, :]).astype(jnp.int32)
    csum = jnp.cumsum(onehot, axis=0)
    rank = jnp.take_along_axis(csum, eid[:, None], axis=1)[:, 0] - 1
    counts = csum[-1]
    pcounts = (counts + bm - 1) // bm * bm
    pend = jnp.cumsum(pcounts)
    poffs = pend - pcounts
    dest = poffs[eid] + rank
    n_blocks = -(-n_assign // bm) + N_EXPERTS
    blk_e = jnp.minimum(jnp.searchsorted(pend, jnp.arange(n_blocks) * bm, side='right'), N_EXPERTS - 1).astype(jnp.int32)
    src_tok = jnp.zeros((n_blocks * bm,), jnp.int32).at[dest].set(jnp.arange(n_assign, dtype=jnp.int32) // TOP_K)
    n_used = (pend[-1] // bm).astype(jnp.int32).reshape(1)
    return dest.astype(jnp.int32), src_tok, blk_e, n_used, n_blocks


def _expert_kernel(blk_e_ref, src_ref, n_used_ref, h_hbm, wg_ref, wu_ref, wd_ref, ys_ref, xbuf, sem, wgu_bf, wd_bf):
    b = pl.program_id(0)
    nb = pl.num_programs(0)
    bm, d = ys_ref.shape
    ff = wg_ref.shape[1]

    def row_copy(blk, slot, r):
        tok = src_ref[blk * bm + r]
        return pltpu.make_async_copy(h_hbm.at[pl.ds(tok, 1), :], xbuf.at[slot, pl.ds(r, 1), :], sem.at[slot])

    def issue(blk, slot):
        def body(r, carry):
            row_copy(blk, slot, r).start()
            return carry
        lax.fori_loop(0, bm, body, 0)

    @pl.when(b == 0)
    def _():
        issue(0, 0)

    @pl.when(b + 1 < nb)
    def _():
        issue(b + 1, (b + 1) % 2)

    slot = b % 2
    pltpu.make_async_copy(h_hbm.at[pl.ds(0, bm), :], xbuf.at[slot], sem.at[slot]).wait()

    changed = jnp.logical_or(b == 0, blk_e_ref[b] != blk_e_ref[jnp.maximum(b - 1, 0)])

    @pl.when(changed)
    def _():
        wgu_bf[:, :ff] = wg_ref[...].astype(BF16)
        wgu_bf[:, ff:] = wu_ref[...].astype(BF16)
        wd_bf[...] = wd_ref[...].astype(BF16)

    @pl.when(b < n_used_ref[0])
    def _():
        xb = xbuf[slot].astype(BF16)
        gu = jnp.dot(xb, wgu_bf[...], preferred_element_type=F32)
        gate = gu[:, :ff]
        act = gate / (1.0 + jnp.exp(-gate)) * gu[:, ff:]
        ys_ref[...] = jnp.dot(act.astype(BF16), wd_bf[...], preferred_element_type=F32)

    @pl.when(b >= n_used_ref[0])
    def _():
        ys_ref[...] = jnp.zeros_like(ys_ref)


def _experts(h2, blk_e, src_tok, n_used, n_blocks, w_gate, w_up, w_down, layer, bm):
    d = h2.shape[1]
    ff = w_gate.shape[-1]
    return pl.pallas_call(
        _expert_kernel,
        out_shape=jax.ShapeDtypeStruct((n_blocks * bm, d), F32),
        grid_spec=pltpu.PrefetchScalarGridSpec(
            num_scalar_prefetch=3,
            grid=(n_blocks,),
            in_specs=[
                pl.BlockSpec(memory_space=pl.ANY),
                pl.BlockSpec((None, None, d, ff), lambda b, be, st, nu: (layer, be[b], 0, 0)),
                pl.BlockSpec((None, None, d, ff), lambda b, be, st, nu: (layer, be[b], 0, 0)),
                pl.BlockSpec((None, None, ff, d), lambda b, be, st, nu: (layer, be[b], 0, 0)),
            ],
            out_specs=pl.BlockSpec((bm, d), lambda b, be, st, nu: (b, 0)),
            scratch_shapes=[
                pltpu.VMEM((2, bm, d), F32),
                pltpu.SemaphoreType.DMA((2,)),
                pltpu.VMEM((d, 2 * ff), BF16),
                pltpu.VMEM((ff, d), BF16),
            ],
        ),
        compiler_params=_cparams(1),
        name="moe_experts",
    )(blk_e, src_tok, n_used, h2, w_gate, w_up, w_down)


def _combine_kernel(dest_ref, ys_hbm, x1_ref, route_ref, mod_ref, g_ref, b_ref, o_ref, ybuf, sem, *, alpha):
    i = pl.program_id(0)
    n = pl.num_programs(0)
    tg = x1_ref.shape[0]

    def issue(tile, slot):
        def body(r, carry):
            for k in range(TOP_K):
                dst_row = dest_ref[(tile * tg + r) * TOP_K + k]
                pltpu.make_async_copy(ys_hbm.at[pl.ds(dst_row, 1), :], ybuf.at[slot, k, pl.ds(r, 1), :], sem.at[slot]).start()
            return carry
        lax.fori_loop(0, tg, body, 0)

    @pl.when(i == 0)
    def _():
        issue(0, 0)

    @pl.when(i + 1 < n)
    def _():
        issue(i + 1, (i + 1) % 2)

    slot = i % 2
    for k in range(TOP_K):
        pltpu.make_async_copy(ys_hbm.at[pl.ds(0, tg), :], ybuf.at[slot, k], sem.at[slot]).wait()
    route = route_ref[...]
    y = route[:, 2:3] * ybuf[slot, 0] + route[:, 3:4] * ybuf[slot, 1]
    o_ref[...] = _layer_norm(alpha * x1_ref[...] + mod_ref[5:6, :] * y, g_ref, b_ref)


def _combine(ys, dest, x1, route, mods, ln_g, ln_b, dims, n_tok, alpha):
    n_batch, seq, ctx_len = dims
    d = x1.shape[1]
    tg = COMBINE_TILE
    n_lat_tiles = n_batch * seq // tg
    tpb = seq // tg
    seg = functools.partial(_seg_of_tile, n_lat_tiles=n_lat_tiles, tiles_per_batch=tpb, n_batch=n_batch)
    return pl.pallas_call(
        functools.partial(_combine_kernel, alpha=alpha),
        out_shape=jax.ShapeDtypeStruct((n_tok, d), F32),
        grid_spec=pltpu.PrefetchScalarGridSpec(
            num_scalar_prefetch=1,
            grid=(n_tok // tg,),
            in_specs=[
                pl.BlockSpec(memory_space=pl.ANY),
                pl.BlockSpec((tg, d), lambda i, ds: (i, 0)),
                pl.BlockSpec((tg, ROUTE_W), lambda i, ds: (i, 0)),
                pl.BlockSpec((None, 6, d), lambda i, ds: (seg(i), 0, 0)),
                pl.BlockSpec(ln_g.shape, lambda i, ds: (0, 0)),
                pl.BlockSpec(ln_b.shape, lambda i, ds: (0, 0)),
            ],
            out_specs=pl.BlockSpec((tg, d), lambda i, ds: (i, 0)),
            scratch_shapes=[pltpu.VMEM((2, TOP_K, tg, d), F32), pltpu.SemaphoreType.DMA((2,))],
        ),
        compiler_params=_cparams(1),
        name="moe_combine",
    )(dest, ys, x1, route, mods, ln_g, ln_b)


def kernel(x, c, ctx, c_ctx, w_ada, b_ada, ln1_g, ln1_b, ln2_g, ln2_b, attn_w_qkv, attn_w_o, attn_sink, conv_w_in, conv_w,
           conv_w_out, nat_w_qkv, nat_w_o, nat_rpb, router_w_group, router_b_group, router_w_expert, router_b_expert,
           expert_w_gate, expert_w_up, expert_w_down):
    n_batch, seq, d = x.shape
    ctx_len = ctx.shape[1]
    depth = w_ada.shape[0]
    dims = (n_batch, seq, ctx_len)
    t_lat = n_batch * seq
    t_all = t_lat + n_batch * ctx_len
    alpha = float((2 * depth) ** 0.25)
    assert d == D_MODEL and n_batch + 1 <= ADA_ROWS
    assert seq % TOKEN_TILE == 0 and (n_batch * ctx_len) % TOKEN_TILE == 0 and seq % GRID_W == 0
    assert TOKEN_TILE % ctx_len == 0 or ctx_len % TOKEN_TILE == 0

    cc = jnp.zeros((ADA_ROWS, d), F32).at[:n_batch].set(c).at[n_batch].set(c_ctx)
    mods_all = _ada(cc, w_ada, b_ada).reshape(depth, ADA_ROWS, 6, d)
    rope_tabs = _rope_tables(seq, TOKEN_TILE)
    xs = jnp.concatenate([x.reshape(t_lat, d), ctx.reshape(n_batch * ctx_len, d)], axis=0)

    n_route_pad = LANES - N_GROUPS - N_EXPERTS
    for i in range(depth):
        last = i == depth - 1
        j = i // 3
        kind = i % 3
        mods = mods_all[i]
        n_tok = t_lat if last else t_all
        g1, b1 = ln1_g[i].reshape(1, d), ln1_b[i].reshape(1, d)
        g2, b2 = ln2_g[i].reshape(1, d), ln2_b[i].reshape(1, d)
        wr = jnp.concatenate([router_w_group[i], router_w_expert[i], jnp.zeros((d, n_route_pad), F32)], axis=1)
        br = jnp.concatenate([router_b_group[i], router_b_expert[i], jnp.zeros((n_route_pad,), F32)]).reshape(1, LANES)

        if kind == 0:
            q, k, v = _proj_in(0, xs, mods, attn_w_qkv[j].astype(BF16), dims, rope_tabs)
            o = _window_attn(q, k, v, attn_sink[j], dims)
            if not last:
                o = jnp.concatenate([o, _ctx_attn(q, k, v, attn_sink[j], dims, A_KV_HEADS, True)], axis=0)
            x1, h2, route = _post_mixer_call(0, o, xs, mods, g1, b1, attn_w_o[j].astype(BF16), wr, br, dims, n_tok, alpha)
        elif kind == 1:
            gb, z = _proj_in(1, xs, mods, conv_w_in[j].astype(BF16), dims)
            x1, h2, route = _post_mixer_call(1, (gb, z), xs, mods, g1, b1, conv_w_out[j].astype(BF16), wr, br, dims, n_tok,
                                             alpha, conv_w=conv_w[j])
        else:
            q, k, v = _proj_in(2, xs, mods, nat_w_qkv[j].astype(BF16), dims)
            o = _nat_attn(q, k, v, nat_rpb[j], dims)
            if not last:
                o = jnp.concatenate([o, _ctx_attn(q, k, v, jnp.zeros((N_HEADS,), F32), dims, N_HEADS, False)], axis=0)
            x1, h2, route = _post_mixer_call(2, o, xs, mods, g1, b1, nat_w_o[j].astype(BF16), wr, br, dims, n_tok, alpha)

        dest, src_tok, blk_e, n_used, n_blocks = _dispatch_plan(route, EXPERT_BLOCK)
        ys = _experts(h2, blk_e, src_tok, n_used, n_blocks, expert_w_gate, expert_w_up, expert_w_down, i, EXPERT_BLOCK)
        xs = _combine(ys, dest, x1, route, mods, g2, b2, dims, n_tok, alpha)
    return xs[:t_lat].reshape(n_batch, seq, d)
```

```python
import functools

import jax
import jax.numpy as jnp
from jax import lax
from jax.experimental import pallas as pl
from jax.experimental.pallas import tpu as pltpu

F32 = jnp.float32
BF16 = jnp.bfloat16
HIGHEST = lax.Precision.HIGHEST

D_MODEL = 1024
HEAD_DIM = 64
N_HEADS = D_MODEL // HEAD_DIM
A_KV_HEADS = N_HEADS // 4
A_KV = A_KV_HEADS * HEAD_DIM
A_WINDOW = 128
A_BLOCK = 128
GRID_W = 64
ROPE_BASE = 10000.0
NA_KH_MAX = 8
NA_KW = 16
N_GROUPS = 4
EXPERTS_PER_GROUP = 8
N_EXPERTS = N_GROUPS * EXPERTS_PER_GROUP
TOP_K = 2
EXPERT_FF = D_MODEL // 2
LN_EPS = 1e-5
NEG_INF = -1e30

LANES = 128
ADA_ROWS = 24
ROUTE_W = 8
TOKEN_TILE = 512
MOE_TILE = 256
EXPERT_BLOCK = 256
CHUNK = 8
TRASH_ROWS = 32
VMEM_LIMIT = 56 * 1024 * 1024


def _cparams(n_axes=1):
    return pltpu.CompilerParams(dimension_semantics=("arbitrary",) * n_axes, vmem_limit_bytes=VMEM_LIMIT)


def _ada_kernel(c_ref, w_ref, b_ref, o_ref):
    cc = c_ref[...]
    s = cc / (1.0 + jnp.exp(-cc))
    o_ref[...] = jnp.dot(s, w_ref[...], precision=HIGHEST, preferred_element_type=F32) + b_ref[...]


def _ada(cc, w_ada, b_ada):
    depth, d, n = w_ada.shape
    nt = n // 4
    return pl.pallas_call(
        _ada_kernel,
        out_shape=jax.ShapeDtypeStruct((depth, ADA_ROWS, n), F32),
        grid=(depth, n // nt),
        in_specs=[
            pl.BlockSpec((ADA_ROWS, d), lambda i, j: (0, 0)),
            pl.BlockSpec((None, d, nt), lambda i, j: (i, 0, j)),
            pl.BlockSpec((None, 1, nt), lambda i, j: (i, 0, j)),
        ],
        out_specs=pl.BlockSpec((None, ADA_ROWS, nt), lambda i, j: (i, 0, j)),
        compiler_params=_cparams(2),
        name="ada_mod",
    )(cc, w_ada, b_ada.reshape(depth, 1, n))


def _modulated(x_ref, mod_ref, shift_row):
    return x_ref[...] * (1.0 + mod_ref[shift_row + 1:shift_row + 2, :]) + mod_ref[shift_row:shift_row + 1, :]


def _layer_norm(r, g_ref, b_ref):
    mu = jnp.mean(r, axis=-1, keepdims=True)
    rc = r - mu
    var = jnp.mean(rc * rc, axis=-1, keepdims=True)
    return rc * lax.rsqrt(var + LN_EPS) * g_ref[...] + b_ref[...]


def _dot_nt(a, b):
    return lax.dot_general(a, b, (((1,), (1,)), ((), ())), preferred_element_type=F32)


def _softmax_pv(s, v, sink_col=None):
    m = jnp.max(s, axis=-1, keepdims=True)
    if sink_col is not None:
        m = jnp.maximum(m, sink_col)
    e = jnp.exp(s - m)
    den = jnp.sum(e, axis=-1, keepdims=True)
    if sink_col is not None:
        den = den + jnp.exp(sink_col - m)
    return jnp.dot(e.astype(BF16), v, preferred_element_type=F32) / den


def _seg_of_tile(i, n_lat_tiles, tiles_per_batch, n_batch):
    return jnp.where(i < n_lat_tiles, i // tiles_per_batch, n_batch)


def _proj_attn_kernel(x_ref, mod_ref, cos_ref, sin_ref, w_ref, q_ref, k_ref, v_ref):
    h = _modulated(x_ref, mod_ref, 0).astype(BF16)
    cos = cos_ref[...]
    sin = sin_ref[...]
    lane = lax.broadcasted_iota(jnp.int32, cos.shape, 1)
    low_half = (lane % 32) < 16

    def rope(a):
        partner = jnp.where(low_half, pltpu.roll(a, LANES - 16, 1), pltpu.roll(a, 16, 1))
        return a * cos + partner * sin

    nq = q_ref.shape[1]
    nk = k_ref.shape[1]
    for j in range(nq // LANES):
        a = jnp.dot(h, w_ref[:, j * LANES:(j + 1) * LANES], preferred_element_type=F32)
        q_ref[:, j * LANES:(j + 1) * LANES] = (rope(a) * (HEAD_DIM ** -0.5)).astype(BF16)
    for j in range(nk // LANES):
        a = jnp.dot(h, w_ref[:, nq + j * LANES:nq + (j + 1) * LANES], preferred_element_type=F32)
        k_ref[:, j * LANES:(j + 1) * LANES] = rope(a).astype(BF16)
    v_ref[...] = jnp.dot(h, w_ref[:, nq + nk:], preferred_element_type=F32).astype(BF16)


def _proj_nat_kernel(x_ref, mod_ref, w_ref, q_ref, k_ref, v_ref):
    h = _modulated(x_ref, mod_ref, 0).astype(BF16)
    d = q_ref.shape[1]
    q_ref[...] = (jnp.dot(h, w_ref[:, :d], preferred_element_type=F32) * (HEAD_DIM ** -0.5)).astype(BF16)
    k_ref[...] = jnp.dot(h, w_ref[:, d:2 * d], preferred_element_type=F32).astype(BF16)
    v_ref[...] = jnp.dot(h, w_ref[:, 2 * d:], preferred_element_type=F32).astype(BF16)


def _proj_conv_kernel(x_ref, mod_ref, w_ref, gb_ref, z_ref):
    h = _modulated(x_ref, mod_ref, 0).astype(BF16)
    d = gb_ref.shape[1]
    gb_ref[...] = jnp.dot(h, w_ref[:, :d], preferred_element_type=F32).astype(BF16)
    gc = jnp.dot(h, w_ref[:, d:2 * d], preferred_element_type=F32)
    u = jnp.dot(h, w_ref[:, 2 * d:], preferred_element_type=F32)
    z_ref[...] = (gc * u).astype(BF16)


def _proj_in(kind, xs, mods, w_bf, dims, rope_tabs=None):
    n_batch, seq, ctx_len = dims
    t, d = xs.shape
    tm = TOKEN_TILE
    n_lat_tiles = n_batch * seq // tm
    tpb = seq // tm
    seg = functools.partial(_seg_of_tile, n_lat_tiles=n_lat_tiles, tiles_per_batch=tpb, n_batch=n_batch)
    x_spec = pl.BlockSpec((tm, d), lambda i: (i, 0))
    mod_spec = pl.BlockSpec((None, 6, d), lambda i: (seg(i), 0, 0))
    w_spec = pl.BlockSpec(w_bf.shape, lambda i: (0, 0))

    def tok_out(n):
        return jax.ShapeDtypeStruct((t, n), BF16), pl.BlockSpec((tm, n), lambda i: (i, 0))

    if kind == 0:
        cos, sin = rope_tabs
        tab_spec = pl.BlockSpec((tm, LANES), lambda i: (jnp.where(i < n_lat_tiles, i % tpb, tpb), 0))
        outs = [tok_out(d), tok_out(A_KV), tok_out(A_KV)]
        body, ins, in_specs = _proj_attn_kernel, (xs, mods, cos, sin, w_bf), [x_spec, mod_spec, tab_spec, tab_spec, w_spec]
    elif kind == 1:
        outs = [tok_out(d), tok_out(d)]
        body, ins, in_specs = _proj_conv_kernel, (xs, mods, w_bf), [x_spec, mod_spec, w_spec]
    else:
        outs = [tok_out(d), tok_out(d), tok_out(d)]
        body, ins, in_specs = _proj_nat_kernel, (xs, mods, w_bf), [x_spec, mod_spec, w_spec]
    return pl.pallas_call(
        body,
        out_shape=[o[0] for o in outs],
        grid=(t // tm,),
        in_specs=in_specs,
        out_specs=[o[1] for o in outs],
        compiler_params=_cparams(1),
        name=f"proj_in_{kind}",
    )(*ins)


def _rope_tables(seq, tm):
    quarter = HEAD_DIM // 4
    inv_freq = ROPE_BASE ** (-jnp.arange(quarter, dtype=F32) / quarter)
    tpos = jnp.arange(seq)
    ang_r = (tpos // GRID_W).astype(F32)[:, None] * inv_freq
    ang_c = (tpos % GRID_W).astype(F32)[:, None] * inv_freq
    cos_h = jnp.concatenate([jnp.cos(ang_r), jnp.cos(ang_r), jnp.cos(ang_c), jnp.cos(ang_c)], axis=1)
    sin_h = jnp.concatenate([-jnp.sin(ang_r), jnp.sin(ang_r), -jnp.sin(ang_c), jnp.sin(ang_c)], axis=1)
    reps = LANES // HEAD_DIM
    cos = jnp.concatenate([jnp.tile(cos_h, (1, reps)), jnp.ones((tm, LANES), F32)], axis=0)
    sin = jnp.concatenate([jnp.tile(sin_h, (1, reps)), jnp.zeros((tm, LANES), F32)], axis=0)
    return cos, sin


def _window_attn_kernel(sink_ref, q_ref, kp_ref, kc_ref, kn_ref, kx_ref, vp_ref, vc_ref, vn_ref, vx_ref, o_ref, *, seq):
    qi = pl.program_id(1)
    blk = A_BLOCK
    group = N_HEADS // A_KV_HEADS
    kcat = jnp.concatenate([kp_ref[...], kc_ref[...], kn_ref[...], kx_ref[...]], axis=0)
    vcat = jnp.concatenate([vp_ref[...], vc_ref[...], vn_ref[...], vx_ref[...]], axis=0)
    nkeys = kcat.shape[0]
    rows = group * blk
    row = lax.broadcasted_iota(jnp.int32, (rows, nkeys), 0) & (blk - 1)
    col = lax.broadcasted_iota(jnp.int32, (rows, nkeys), 1)
    kpos = qi * blk - blk + col
    rel = col - blk - row
    local_ok = (rel >= -A_WINDOW) & (rel <= A_WINDOW) & (kpos >= 0) & (kpos < seq)
    valid = local_ok | (col >= 3 * blk)
    for kk in range(A_KV_HEADS):
        k_h = kcat[:, kk * HEAD_DIM:(kk + 1) * HEAD_DIM]
        v_h = vcat[:, kk * HEAD_DIM:(kk + 1) * HEAD_DIM]
        heads = [kk * group + g for g in range(group)]
        q_st = jnp.concatenate([q_ref[:, h * HEAD_DIM:(h + 1) * HEAD_DIM] for h in heads], axis=0)
        sink_col = jnp.concatenate([jnp.full((blk, 1), sink_ref[h], F32) for h in heads], axis=0)
        s = jnp.where(valid, _dot_nt(q_st, k_h), NEG_INF)
        o = _softmax_pv(s, v_h, sink_col).astype(o_ref.dtype)
        for g, h in enumerate(heads):
            o_ref[:, h * HEAD_DIM:(h + 1) * HEAD_DIM] = o[g * blk:(g + 1) * blk, :]


def _window_attn(q, k, v, sink, dims):
    n_batch, seq, ctx_len = dims
    t, d = q.shape
    blk = A_BLOCK
    nb = seq // blk
    t_lat = n_batch * seq
    ctx_blk0 = t_lat // ctx_len

    def q_map(b, i, s):
        return (b * nb + i, 0)

    def kv_map(off):
        return lambda b, i, s: (b * nb + jnp.clip(i + off, 0, nb - 1), 0)

    def ctx_map(b, i, s):
        return (ctx_blk0 + b, 0)

    kv_specs = [pl.BlockSpec((blk, A_KV), kv_map(-1)), pl.BlockSpec((blk, A_KV), kv_map(0)),
                pl.BlockSpec((blk, A_KV), kv_map(1)), pl.BlockSpec((ctx_len, A_KV), ctx_map)]
    return pl.pallas_call(
        functools.partial(_window_attn_kernel, seq=seq),
        out_shape=jax.ShapeDtypeStruct((t_lat, d), BF16),
        grid_spec=pltpu.PrefetchScalarGridSpec(
            num_scalar_prefetch=1,
            grid=(n_batch, nb),
            in_specs=[pl.BlockSpec((blk, d), q_map)] + kv_specs + kv_specs,
            out_specs=pl.BlockSpec((blk, d), q_map),
        ),
        compiler_params=_cparams(2),
        name="window_attn",
    )(sink, q, k, k, k, k, v, v, v, v)


def _ctx_attn_kernel(sink_ref, q_ref, k_ref, v_ref, o_ref, *, n_kv, use_sink):
    group = N_HEADS // n_kv
    n = q_ref.shape[0]
    for kk in range(n_kv):
        k_h = k_ref[:, kk * HEAD_DIM:(kk + 1) * HEAD_DIM]
        v_h = v_ref[:, kk * HEAD_DIM:(kk + 1) * HEAD_DIM]
        heads = [kk * group + g for g in range(group)]
        q_st = jnp.concatenate([q_ref[:, h * HEAD_DIM:(h + 1) * HEAD_DIM] for h in heads], axis=0)
        sink_col = None
        if use_sink:
            sink_col = jnp.concatenate([jnp.full((n, 1), sink_ref[h], F32) for h in heads], axis=0)
        o = _softmax_pv(_dot_nt(q_st, k_h), v_h, sink_col).astype(o_ref.dtype)
        for g, h in enumerate(heads):
            o_ref[:, h * HEAD_DIM:(h + 1) * HEAD_DIM] = o[g * n:(g + 1) * n, :]


def _ctx_attn(q, k, v, sink, dims, n_kv, use_sink):
    n_batch, seq, ctx_len = dims
    d = q.shape[1]
    kvw = n_kv * HEAD_DIM
    blk0 = n_batch * seq // ctx_len
    return pl.pallas_call(
        functools.partial(_ctx_attn_kernel, n_kv=n_kv, use_sink=use_sink),
        out_shape=jax.ShapeDtypeStruct((n_batch * ctx_len, d), BF16),
        grid_spec=pltpu.PrefetchScalarGridSpec(
            num_scalar_prefetch=1,
            grid=(n_batch,),
            in_specs=[pl.BlockSpec((ctx_len, d), lambda b, s: (blk0 + b, 0)),
                      pl.BlockSpec((ctx_len, kvw), lambda b, s: (blk0 + b, 0)),
                      pl.BlockSpec((ctx_len, kvw), lambda b, s: (blk0 + b, 0))],
            out_specs=pl.BlockSpec((ctx_len, d), lambda b, s: (b, 0)),
        ),
        compiler_params=_cparams(1),
        name="ctx_attn",
    )(sink, q, k, v)


def _nat_kernel(q_ref, k_ref, v_ref, kx_ref, vx_ref, bias_ref, o_ref, *, kh, n_rows):
    r = pl.program_id(1)
    rs = jnp.clip(r - kh // 2, 0, n_rows - kh)
    start = pl.multiple_of(rs * GRID_W, GRID_W)
    n_loc = kh * GRID_W
    for h in range(N_HEADS):
        sl = slice(h * HEAD_DIM, (h + 1) * HEAD_DIM)
        q_h = q_ref[:, sl]
        k_loc = k_ref[pl.ds(start, n_loc), sl]
        v_loc = v_ref[pl.ds(start, n_loc), sl]
        s_loc = _dot_nt(q_h, k_loc) + bias_ref[h]
        s_ctx = _dot_nt(q_h, kx_ref[:, sl])
        s = jnp.concatenate([s_loc, s_ctx], axis=-1)
        v_all = jnp.concatenate([v_loc, vx_ref[:, sl]], axis=0)
        o_ref[:, sl] = _softmax_pv(s, v_all).astype(o_ref.dtype)


def _nat_bias_table(rpb, kh):
    qc = jnp.arange(GRID_W)
    kc = jnp.arange(GRID_W)
    q_start = jnp.clip(qc - NA_KW // 2, 0, GRID_W - NA_KW)
    col_ok = (kc[None, :] >= q_start[:, None]) & (kc[None, :] < q_start[:, None] + NA_KW)
    dc = jnp.clip(kc[None, :] - qc[:, None] + NA_KW - 1, 0, 2 * NA_KW - 2)
    case = jnp.arange(kh)
    j = jnp.arange(kh)
    dr = j[None, :] - case[:, None] + NA_KH_MAX - 1
    b = rpb.astype(F32)[:, dr][:, :, :, dc]
    b = jnp.where(col_ok[None, None, None], b, NEG_INF)
    b = b.transpose(1, 0, 3, 2, 4)
    return b.reshape(kh, N_HEADS, GRID_W, kh * GRID_W)


def _nat_attn(q, k, v, rpb, dims):
    n_batch, seq, ctx_len = dims
    t, d = q.shape
    n_rows = seq // GRID_W
    kh = min(NA_KH_MAX, n_rows)
    bias = _nat_bias_table(rpb, kh)
    ctx_blk0 = n_batch * seq // ctx_len

    def case_of(r):
        return r - jnp.clip(r - kh // 2, 0, n_rows - kh)

    return pl.pallas_call(
        functools.partial(_nat_kernel, kh=kh, n_rows=n_rows),
        out_shape=jax.ShapeDtypeStruct((n_batch * seq, d), BF16),
        grid=(n_batch, n_rows),
        in_specs=[
            pl.BlockSpec((GRID_W, d), lambda b, r: (b * n_rows + r, 0)),
            pl.BlockSpec((seq, d), lambda b, r: (b, 0)),
            pl.BlockSpec((seq, d), lambda b, r: (b, 0)),
            pl.BlockSpec((ctx_len, d), lambda b, r: (ctx_blk0 + b, 0)),
            pl.BlockSpec((ctx_len, d), lambda b, r: (ctx_blk0 + b, 0)),
            pl.BlockSpec((None, N_HEADS, GRID_W, kh * GRID_W), lambda b, r: (case_of(r), 0, 0, 0)),
        ],
        out_specs=pl.BlockSpec((GRID_W, d), lambda b, r: (b * n_rows + r, 0)),
        compiler_params=_cparams(2),
        name="nat_attn",
    )(q, k, v, k, v, bias)


def _route(h2, wr_ref, br_ref):
    logits = jnp.dot(h2, wr_ref[...], precision=HIGHEST, preferred_element_type=F32) + br_ref[...]
    lane_i = lax.broadcasted_iota(jnp.int32, logits.shape, 1)
    lane = lane_i.astype(F32)
    big = jnp.float32(1e9)
    low = jnp.float32(-3e38)
    is_group = lane_i < N_GROUPS
    gl = jnp.where(is_group, logits, low)
    gmax = jnp.max(gl, axis=-1, keepdims=True)
    g_idx = jnp.min(jnp.where(gl == gmax, lane, big), axis=-1, keepdims=True)
    gsum = jnp.sum(jnp.where(is_group, jnp.exp(gl - gmax), 0.0), axis=-1, keepdims=True)
    g_prob = 1.0 / gsum
    e_lo = N_GROUPS + g_idx * EXPERTS_PER_GROUP
    in_group = (lane >= e_lo) & (lane < e_lo + EXPERTS_PER_GROUP)
    el = jnp.where(in_group, logits, low)
    v1 = jnp.max(el, axis=-1, keepdims=True)
    i1 = jnp.min(jnp.where(el == v1, lane, big), axis=-1, keepdims=True)
    el2 = jnp.where(lane == i1, low, el)
    v2 = jnp.max(el2, axis=-1, keepdims=True)
    i2 = jnp.min(jnp.where(el2 == v2, lane, big), axis=-1, keepdims=True)
    t = jnp.exp(v2 - v1)
    p1 = 1.0 / (1.0 + t)
    p2 = t / (1.0 + t)
    return i1 - N_GROUPS, i2 - N_GROUPS, p1 * g_prob, p2 * g_prob


def _tile_slots(e0, e1):
    mt = e0.shape[0]
    lane = lax.broadcasted_iota(jnp.int32, (mt, LANES), 1).astype(F32)
    hit0 = lane == e0
    hit1 = lane == e1
    onehot = jnp.where(hit0, 1.0, jnp.where(hit1, 1.0, 0.0))
    r_i = lax.broadcasted_iota(jnp.int32, (mt, mt), 0)
    c_i = lax.broadcasted_iota(jnp.int32, (mt, mt), 1)
    tri = jnp.where(c_i < r_i, 1.0, 0.0).astype(BF16)
    before = jnp.dot(tri, onehot.astype(BF16), preferred_element_type=F32)
    cnt = jnp.sum(onehot, axis=0, keepdims=True)
    padded = jnp.floor((cnt + (CHUNK - 1)) * (1.0 / CHUNK)) * CHUNK
    a_i = lax.broadcasted_iota(jnp.int32, (LANES, LANES), 0)
    b_i = lax.broadcasted_iota(jnp.int32, (LANES, LANES), 1)
    upper = jnp.where(a_i < b_i, 1.0, 0.0)
    goff = jnp.dot(jnp.broadcast_to(padded, (8, LANES)), upper, precision=HIGHEST, preferred_element_type=F32)[0:1]
    pos = before + goff
    slot0 = jnp.sum(jnp.where(hit0, pos, 0.0), axis=-1, keepdims=True)
    slot1 = jnp.sum(jnp.where(hit1, pos, 0.0), axis=-1, keepdims=True)
    return slot0, slot1, cnt, goff


def _columns_to_rows(cols):
    mt = cols[0].shape[0]
    lane = lax.broadcasted_iota(jnp.int32, (mt, LANES), 1)
    packed = jnp.zeros((mt, LANES), F32)
    for j, col in enumerate(cols):
        packed = jnp.where(lane == j, col, packed)
    sel = jnp.where(lax.broadcasted_iota(jnp.int32, (8, LANES), 0) == lax.broadcasted_iota(jnp.int32, (8, LANES), 1), 1.0, 0.0)
    return lax.dot_general(sel, packed, (((1,), (1,)), ((), ())), precision=HIGHEST, preferred_element_type=F32)


def _pack_bf16_pairs(a):
    n = a.shape[1] // 2
    lo = lax.bitcast_convert_type(a[:, :n], jnp.uint32) >> 16
    hi = lax.bitcast_convert_type(a[:, n:], jnp.uint32) & jnp.uint32(0xFFFF0000)
    return lo | hi


def _unpack_bf16_pairs(w):
    a = lax.bitcast_convert_type(w << 16, F32).astype(BF16)
    b = lax.bitcast_convert_type(w & jnp.uint32(0xFFFF0000), F32).astype(BF16)
    return a, b


def _post_mixer(o_bf, x_ref, mod_ref, g_ref, b_ref, wo_ref, wr_ref, br_ref, x1_ref, xs_ref, route_ref, meta_ref, alpha):
    y = jnp.dot(o_bf, wo_ref[...], preferred_element_type=F32)
    x1 = _layer_norm(alpha * x_ref[...] + mod_ref[2:3, :] * y, g_ref, b_ref)
    x1_ref[...] = x1
    h2 = x1 * (1.0 + mod_ref[4:5, :]) + mod_ref[3:4, :]
    e0, e1, g0, g1 = _route(h2, wr_ref, br_ref)
    slot0, slot1, cnt, goff = _tile_slots(e0, e1)

    rc = xs_ref.shape[0]
    mt = h2.shape[0]
    slot_rows = _columns_to_rows([slot0, slot1])
    s_i = lax.broadcasted_iota(jnp.int32, (rc, mt), 0).astype(F32)
    pick = jnp.where(s_i == slot_rows[0:1, :], 1.0, jnp.where(s_i == slot_rows[1:2, :], 1.0, 0.0)).astype(BF16)
    compact = jnp.dot(pick, h2.astype(BF16), preferred_element_type=F32)
    xs_ref[...] = _pack_bf16_pairs(compact)

    lane = lax.broadcasted_iota(jnp.int32, (mt, LANES), 1)
    rec = jnp.zeros((mt, LANES), F32)
    for j, col in enumerate([e0, e1, g0, g1, slot0, slot1]):
        rec = jnp.where(lane == j, col, rec)
    route_ref[...] = rec[:, :ROUTE_W]
    sub = lax.broadcasted_iota(jnp.int32, (8, LANES), 0)
    meta_ref[...] = jnp.where(sub == 0, cnt, jnp.where(sub == 1, goff, 0.0))


def _post_attn_kernel(o_ref, x_ref, mod_ref, g_ref, b_ref, wo_ref, wr_ref, br_ref, x1_ref, xs_ref, route_ref, meta_ref, *, alpha):
    _post_mixer(o_ref[...], x_ref, mod_ref, g_ref, b_ref, wo_ref, wr_ref, br_ref, x1_ref, xs_ref, route_ref, meta_ref, alpha)


def _post_conv_kernel(gb_ref, z_ref, zp_ref, zn_ref, cw_ref, x_ref, mod_ref, g_ref, b_ref, wo_ref, wr_ref, br_ref,
                      x1_ref, xs_ref, route_ref, meta_ref, *, alpha, t_lat, seq, ctx_len, n_tiles):
    i = jnp.minimum(pl.program_id(0), n_tiles - 1)
    tm, d = z_ref.shape
    halo = zp_ref.shape[0]
    z = z_ref[...].astype(F32)
    row = lax.broadcasted_iota(jnp.int32, (tm, 1), 0)
    g = i * tm + row
    pos = jnp.where(g < t_lat, g % seq, g % ctx_len)
    n_seq = jnp.where(g < t_lat, seq, ctx_len)
    prev_row = zp_ref[halo - 1:halo, :].astype(F32)
    next_row = zn_ref[0:1, :].astype(F32)
    z_prev = jnp.where(row == 0, prev_row, pltpu.roll(z, 1, 0))
    z_next = jnp.where(row == tm - 1, next_row, pltpu.roll(z, tm - 1, 0))
    z_prev = jnp.where(pos == 0, 0.0, z_prev)
    z_next = jnp.where(pos == n_seq - 1, 0.0, z_next)
    y = cw_ref[0:1, :] * z_prev + cw_ref[1:2, :] * z + cw_ref[2:3, :] * z_next
    o = (gb_ref[...].astype(F32) * y).astype(BF16)
    _post_mixer(o, x_ref, mod_ref, g_ref, b_ref, wo_ref, wr_ref, br_ref, x1_ref, xs_ref, route_ref, meta_ref, alpha)


def _compact_rows(mt):
    return TOP_K * mt + N_EXPERTS * CHUNK


def _post_mixer_call(kind, mixer_out, xs, mods, ln_g, ln_b, wo_bf, wr, br, dims, n_tok, alpha, conv_w=None):
    n_batch, seq, ctx_len = dims
    d = xs.shape[1]
    tm = MOE_TILE
    rc = _compact_rows(tm)
    n_tiles = n_tok // tm
    n_lat_tiles = n_batch * seq // tm
    tpb = seq // tm
    seg = functools.partial(_seg_of_tile, n_lat_tiles=n_lat_tiles, tiles_per_batch=tpb, n_batch=n_batch)
    tile = lambda i: jnp.minimum(i, n_tiles - 1)
    tok = pl.BlockSpec((tm, d), lambda i: (tile(i), 0))
    full = lambda a: pl.BlockSpec(a.shape, lambda i: (0,) * a.ndim)
    common_ins = (xs, mods, ln_g, ln_b, wo_bf, wr, br)
    common_specs = [tok, pl.BlockSpec((None, 6, d), lambda i: (seg(tile(i)), 0, 0)), full(ln_g), full(ln_b),
                    full(wo_bf), full(wr), full(br)]
    if kind == 1:
        gb, z = mixer_out
        halo = 16
        hb = tm // halo
        last_blk = z.shape[0] // halo - 1
        ins = (gb, z, z, z, conv_w) + common_ins
        in_specs = [tok, tok,
                    pl.BlockSpec((halo, d), lambda i: (jnp.maximum(tile(i) * hb - 1, 0), 0)),
                    pl.BlockSpec((halo, d), lambda i: (jnp.minimum((tile(i) + 1) * hb, last_blk), 0)),
                    full(conv_w)] + common_specs
        body = functools.partial(_post_conv_kernel, alpha=alpha, t_lat=n_batch * seq, seq=seq, ctx_len=ctx_len,
                                 n_tiles=n_tiles)
    else:
        ins = (mixer_out,) + common_ins
        in_specs = [tok] + common_specs
        body = functools.partial(_post_attn_kernel, alpha=alpha)
    return pl.pallas_call(
        body,
        out_shape=[jax.ShapeDtypeStruct((n_tok, d), F32),
                   jax.ShapeDtypeStruct(((n_tiles + 1) * rc, d // 2), jnp.uint32),
                   jax.ShapeDtypeStruct((n_tok, ROUTE_W), F32),
                   jax.ShapeDtypeStruct((n_tiles, 8, LANES), F32)],
        grid=(n_tiles + 1,),
        in_specs=in_specs,
        out_specs=[tok, pl.BlockSpec((rc, d // 2), lambda i: (i, 0)),
                   pl.BlockSpec((tm, ROUTE_W), lambda i: (tile(i), 0)),
                   pl.BlockSpec((None, 8, LANES), lambda i: (tile(i), 0, 0))],
        compiler_params=_cparams(1),
        name=f"post_mixer_{kind}",
    )(*ins)


def _expert_schedule(meta, rc, bm):
    n_tiles = meta.shape[0]
    cpb = bm // CHUNK
    cnt = meta[:, 0, :N_EXPERTS].astype(jnp.int32)
    goff = meta[:, 1, :N_EXPERTS].astype(jnp.int32)
    nch = ((cnt + CHUNK - 1) // CHUNK).T
    per_e = jnp.sum(nch, axis=1)
    nbk = (per_e + cpb - 1) // cpb
    blk_start = jnp.cumsum(nbk) - nbk
    run_start = (blk_start * cpb)[:, None] + jnp.cumsum(nch, axis=1) - nch
    run_start = run_start.reshape(-1)
    run_len = nch.reshape(-1)
    run_src = (jnp.arange(n_tiles, dtype=jnp.int32)[None, :] * rc + goff.T).reshape(-1)
    n_blocks = (TOP_K * n_tiles * MOE_TILE + (CHUNK - 1) * N_EXPERTS * n_tiles) // CHUNK // cpb + N_EXPERTS
    q = jnp.arange(n_blocks * cpb, dtype=jnp.int32)
    run = jnp.clip(jnp.searchsorted(run_start, q, side='right') - 1, 0, run_start.shape[0] - 1)
    k = q - run_start[run]
    chunk_src = jnp.where(k < run_len[run], run_src[run] + CHUNK * k, -1).astype(jnp.int32)
    blk = jnp.arange(n_blocks, dtype=jnp.int32)
    blk_e = jnp.clip(jnp.searchsorted(blk_start, blk, side='right') - 1, 0, N_EXPERTS - 1).astype(jnp.int32)
    n_used = jnp.sum(nbk).astype(jnp.int32).reshape(1)
    return chunk_src, blk_e, n_used, n_blocks


def _expert_kernel(blk_e_ref, src_ref, n_used_ref, x_hbm, wg_ref, wu_ref, wd_ref, y_hbm, xbuf, ybuf, gsem, ssem,
                   wgu_bf, wd_bf, *, zero_row, trash_row):
    b = pl.program_id(0)
    nb = pl.num_programs(0)
    _, bm, half = xbuf.shape
    ff = wg_ref.shape[1]
    cpb = bm // CHUNK
    n_used = n_used_ref[0]

    def gather(blk, slot):
        for j in range(cpb):
            v = src_ref[blk * cpb + j]
            row = pl.multiple_of(jnp.where(v < 0, zero_row, v), CHUNK)
            pltpu.make_async_copy(x_hbm.at[pl.ds(row, CHUNK), :], xbuf.at[slot, pl.ds(j * CHUNK, CHUNK), :],
                                  gsem.at[slot]).start()

    def scatter(blk, slot):
        for j in range(cpb):
            v = src_ref[blk * cpb + j]
            row = pl.multiple_of(jnp.where(v < 0, trash_row + slot * bm + j * CHUNK, v), CHUNK)
            pltpu.make_async_copy(ybuf.at[slot, pl.ds(j * CHUNK, CHUNK), :], y_hbm.at[pl.ds(row, CHUNK), :],
                                  ssem.at[slot]).start()

    def wait_gather(slot):
        pltpu.make_async_copy(x_hbm.at[pl.ds(0, bm), :], xbuf.at[slot], gsem.at[slot]).wait()

    def wait_scatter(slot):
        pltpu.make_async_copy(ybuf.at[slot], y_hbm.at[pl.ds(0, bm), :], ssem.at[slot]).wait()

    @pl.when((b == 0) & (n_used > 0))
    def _():
        gather(0, 0)

    @pl.when(b + 1 < n_used)
    def _():
        gather(b + 1, (b + 1) % 2)

    @pl.when(b < n_used)
    def _():
        slot = b % 2
        wait_gather(slot)

        @pl.when(b >= 2)
        def _():
            wait_scatter(slot)

        @pl.when((b == 0) | (blk_e_ref[b] != blk_e_ref[jnp.maximum(b - 1, 0)]))
        def _():
            wgu_bf[:, :ff] = wg_ref[...].astype(BF16)
            wgu_bf[:, ff:] = wu_ref[...].astype(BF16)
            wd_bf[...] = wd_ref[...].astype(BF16)

        xa, xb = _unpack_bf16_pairs(xbuf[slot])
        gu = (jnp.dot(xa, wgu_bf[:half, :], preferred_element_type=F32)
              + jnp.dot(xb, wgu_bf[half:, :], preferred_element_type=F32))
        gate = gu[:, :ff]
        act = gate / (1.0 + jnp.exp(-gate)) * gu[:, ff:]
        y = jnp.dot(act.astype(BF16), wd_bf[...], preferred_element_type=F32)
        ybuf[slot] = _pack_bf16_pairs(y.astype(BF16).astype(F32))
        scatter(b, slot)

    @pl.when(b == nb - 1)
    def _():
        @pl.when(n_used >= 1)
        def _():
            wait_scatter((n_used - 1) % 2)

        @pl.when(n_used >= 2)
        def _():
            wait_scatter(n_used % 2)


def _experts(xs_c, chunk_src, blk_e, n_used, n_blocks, w_gate, w_up, w_down, layer, rc, bm):
    half = xs_c.shape[1]
    d = 2 * half
    ff = w_gate.shape[-1]
    n_tiles = xs_c.shape[0] // rc - 1
    assert rc - TRASH_ROWS >= TOP_K * MOE_TILE + N_EXPERTS * (CHUNK - 1) and rc >= 2 * bm
    return pl.pallas_call(
        functools.partial(_expert_kernel, zero_row=rc - TRASH_ROWS, trash_row=n_tiles * rc),
        out_shape=jax.ShapeDtypeStruct(xs_c.shape, xs_c.dtype),
        grid_spec=pltpu.PrefetchScalarGridSpec(
            num_scalar_prefetch=3,
            grid=(n_blocks,),
            in_specs=[
                pl.BlockSpec(memory_space=pl.ANY),
                pl.BlockSpec((None, None, d, ff), lambda b, be, cs, nu: (layer, be[b], 0, 0)),
                pl.BlockSpec((None, None, d, ff), lambda b, be, cs, nu: (layer, be[b], 0, 0)),
                pl.BlockSpec((None, None, ff, d), lambda b, be, cs, nu: (layer, be[b], 0, 0)),
            ],
            out_specs=pl.BlockSpec(memory_space=pl.ANY),
            scratch_shapes=[
                pltpu.VMEM((2, bm, half), jnp.uint32),
                pltpu.VMEM((2, bm, half), jnp.uint32),
                pltpu.SemaphoreType.DMA((2,)),
                pltpu.SemaphoreType.DMA((2,)),
                pltpu.VMEM((d, 2 * ff), BF16),
                pltpu.VMEM((ff, d), BF16),
            ],
        ),
        input_output_aliases={3: 0},
        compiler_params=_cparams(1),
        name="moe_experts",
    )(blk_e, chunk_src, n_used, xs_c, w_gate, w_up, w_down)


def _combine_kernel(ys_ref, x1_ref, route_ref, mod_ref, g_ref, b_ref, o_ref, *, alpha):
    rc = ys_ref.shape[0]
    mt = x1_ref.shape[0]
    ya, yb = _unpack_bf16_pairs(ys_ref[...])
    route = route_ref[...]
    s_i = lax.broadcasted_iota(jnp.int32, (mt, rc), 1).astype(F32)
    y = None
    for k in range(TOP_K):
        pick = jnp.where(s_i == route[:, 4 + k:5 + k], 1.0, 0.0).astype(BF16)
        yk = jnp.concatenate([jnp.dot(pick, ya, preferred_element_type=F32),
                              jnp.dot(pick, yb, preferred_element_type=F32)], axis=1)
        yk = route[:, 2 + k:3 + k] * yk
        y = yk if y is None else y + yk
    o_ref[...] = _layer_norm(alpha * x1_ref[...] + mod_ref[5:6, :] * y, g_ref, b_ref)


def _combine(ys_c, x1, route, mods, ln_g, ln_b, dims, n_tok, alpha, rc):
    n_batch, seq, ctx_len = dims
    d = x1.shape[1]
    tg = MOE_TILE
    n_lat_tiles = n_batch * seq // tg
    tpb = seq // tg
    seg = functools.partial(_seg_of_tile, n_lat_tiles=n_lat_tiles, tiles_per_batch=tpb, n_batch=n_batch)
    return pl.pallas_call(
        functools.partial(_combine_kernel, alpha=alpha),
        out_shape=jax.ShapeDtypeStruct((n_tok, d), F32),
        grid=(n_tok // tg,),
        in_specs=[
            pl.BlockSpec((rc, d // 2), lambda i: (i, 0)),
            pl.BlockSpec((tg, d), lambda i: (i, 0)),
            pl.BlockSpec((tg, ROUTE_W), lambda i: (i, 0)),
            pl.BlockSpec((None, 6, d), lambda i: (seg(i), 0, 0)),
            pl.BlockSpec(ln_g.shape, lambda i: (0, 0)),
            pl.BlockSpec(ln_b.shape, lambda i: (0, 0)),
        ],
        out_specs=pl.BlockSpec((tg, d), lambda i: (i, 0)),
        compiler_params=_cparams(1),
        name="moe_combine",
    )(ys_c, x1, route, mods, ln_g, ln_b)


def kernel(x, c, ctx, c_ctx, w_ada, b_ada, ln1_g, ln1_b, ln2_g, ln2_b, attn_w_qkv, attn_w_o, attn_sink, conv_w_in, conv_w,
           conv_w_out, nat_w_qkv, nat_w_o, nat_rpb, router_w_group, router_b_group, router_w_expert, router_b_expert,
           expert_w_gate, expert_w_up, expert_w_down):
    n_batch, seq, d = x.shape
    ctx_len = ctx.shape[1]
    depth = w_ada.shape[0]
    dims = (n_batch, seq, ctx_len)
    t_lat = n_batch * seq
    t_all = t_lat + n_batch * ctx_len
    alpha = float((2 * depth) ** 0.25)
    assert d == D_MODEL and n_batch + 1 <= ADA_ROWS
    assert seq % TOKEN_TILE == 0 and (n_batch * ctx_len) % TOKEN_TILE == 0 and seq % GRID_W == 0
    assert TOKEN_TILE % ctx_len == 0 or ctx_len % TOKEN_TILE == 0
    rc = _compact_rows(MOE_TILE)

    cc = jnp.zeros((ADA_ROWS, d), F32).at[:n_batch].set(c).at[n_batch].set(c_ctx)
    mods_all = _ada(cc, w_ada, b_ada).reshape(depth, ADA_ROWS, 6, d)
    rope_tabs = _rope_tables(seq, TOKEN_TILE)
    xs = jnp.concatenate([x.reshape(t_lat, d), ctx.reshape(n_batch * ctx_len, d)], axis=0)

    n_route_pad = LANES - N_GROUPS - N_EXPERTS
    for i in range(depth):
        last = i == depth - 1
        j = i // 3
        kind = i % 3
        mods = mods_all[i]
        n_tok = t_lat if last else t_all
        g1, b1 = ln1_g[i].reshape(1, d), ln1_b[i].reshape(1, d)
        g2, b2 = ln2_g[i].reshape(1, d), ln2_b[i].reshape(1, d)
        wr = jnp.concatenate([router_w_group[i], router_w_expert[i], jnp.zeros((d, n_route_pad), F32)], axis=1)
        br = jnp.concatenate([router_b_group[i], router_b_expert[i], jnp.zeros((n_route_pad,), F32)]).reshape(1, LANES)

        if kind == 0:
            q, k, v = _proj_in(0, xs, mods, attn_w_qkv[j].astype(BF16), dims, rope_tabs)
            o = _window_attn(q, k, v, attn_sink[j], dims)
            if not last:
                o = jnp.concatenate([o, _ctx_attn(q, k, v, attn_sink[j], dims, A_KV_HEADS, True)], axis=0)
            post = _post_mixer_call(0, o, xs, mods, g1, b1, attn_w_o[j].astype(BF16), wr, br, dims, n_tok, alpha)
        elif kind == 1:
            gb, z = _proj_in(1, xs, mods, conv_w_in[j].astype(BF16), dims)
            post = _post_mixer_call(1, (gb, z), xs, mods, g1, b1, conv_w_out[j].astype(BF16), wr, br, dims, n_tok, alpha,
                                    conv_w=conv_w[j])
        else:
            q, k, v = _proj_in(2, xs, mods, nat_w_qkv[j].astype(BF16), dims)
            o = _nat_attn(q, k, v, nat_rpb[j], dims)
            if not last:
                o = jnp.concatenate([o, _ctx_attn(q, k, v, jnp.zeros((N_HEADS,), F32), dims, N_HEADS, False)], axis=0)
            post = _post_mixer_call(2, o, xs, mods, g1, b1, nat_w_o[j].astype(BF16), wr, br, dims, n_tok, alpha)

        x1, xs_c, route, meta = post
        chunk_src, blk_e, n_used, n_blocks = _expert_schedule(meta, rc, EXPERT_BLOCK)
        ys_c = _experts(xs_c, chunk_src, blk_e, n_used, n_blocks, expert_w_gate, expert_w_up, expert_w_down, i, rc,
                        EXPERT_BLOCK)
        xs = _combine(ys_c, x1, route, mods, g2, b2, dims, n_tok, alpha, rc)
    return xs[:t_lat].reshape(n_batch, seq, d)
```

```python
import functools

import jax
import jax.numpy as jnp
from jax import lax
from jax.experimental import pallas as pl
from jax.experimental.pallas import tpu as pltpu

F32 = jnp.float32
BF16 = jnp.bfloat16
HIGHEST = lax.Precision.HIGHEST

D_MODEL = 1024
HEAD_DIM = 64
N_HEADS = D_MODEL // HEAD_DIM
A_KV_HEADS = N_HEADS // 4
A_KV = A_KV_HEADS * HEAD_DIM
A_WINDOW = 128
A_BLOCK = 128
GRID_W = 64
ROPE_BASE = 10000.0
NA_KH_MAX = 8
NA_KW = 16
N_GROUPS = 4
EXPERTS_PER_GROUP = 8
N_EXPERTS = N_GROUPS * EXPERTS_PER_GROUP
TOP_K = 2
EXPERT_FF = D_MODEL // 2
LN_EPS = 1e-5
NEG_INF = -1e30

LANES = 128
ADA_ROWS = 24
ROUTE_W = 8
TOKEN_TILE = 512
MOE_TILE = 256
EXPERT_BLOCK = 512
EXPERT_PARTS = 2
CHUNK = 8
TRASH_ROWS = 32
ROUTE_ROWS = 64
NAT_ROWS = 4
NAT_WINDOW_ROWS = 12
VMEM_LIMIT = 56 * 1024 * 1024


def _cparams(n_axes=1):
    return pltpu.CompilerParams(dimension_semantics=("arbitrary",) * n_axes, vmem_limit_bytes=VMEM_LIMIT)


def _ada_kernel(c_ref, w_ref, b_ref, o_ref):
    cc = c_ref[...]
    s = cc / (1.0 + jnp.exp(-cc))
    o_ref[...] = jnp.dot(s, w_ref[...], precision=HIGHEST, preferred_element_type=F32) + b_ref[...]


def _ada(cc, w_ada, b_ada):
    depth, d, n = w_ada.shape
    nt = n // 4
    return pl.pallas_call(
        _ada_kernel,
        out_shape=jax.ShapeDtypeStruct((depth, ADA_ROWS, n), F32),
        grid=(depth, n // nt),
        in_specs=[
            pl.BlockSpec((ADA_ROWS, d), lambda i, j: (0, 0)),
            pl.BlockSpec((None, d, nt), lambda i, j: (i, 0, j)),
            pl.BlockSpec((None, 1, nt), lambda i, j: (i, 0, j)),
        ],
        out_specs=pl.BlockSpec((None, ADA_ROWS, nt), lambda i, j: (i, 0, j)),
        compiler_params=_cparams(2),
        name="ada_mod",
    )(cc, w_ada, b_ada.reshape(depth, 1, n))


def _modulated(x_ref, mod_ref, shift_row):
    return x_ref[...] * (1.0 + mod_ref[shift_row + 1:shift_row + 2, :]) + mod_ref[shift_row:shift_row + 1, :]


def _layer_norm(r, g_ref, b_ref):
    mu = jnp.mean(r, axis=-1, keepdims=True)
    rc = r - mu
    var = jnp.mean(rc * rc, axis=-1, keepdims=True)
    return rc * lax.rsqrt(var + LN_EPS) * g_ref[...] + b_ref[...]


def _dot_nt(a, b):
    return lax.dot_general(a, b, (((1,), (1,)), ((), ())), preferred_element_type=F32)


def _softmax_pv(s, v, sink_col=None):
    m = jnp.max(s, axis=-1, keepdims=True)
    if sink_col is not None:
        m = jnp.maximum(m, sink_col)
    e = jnp.exp(s - m)
    den = jnp.sum(e, axis=-1, keepdims=True)
    if sink_col is not None:
        den = den + jnp.exp(sink_col - m)
    return jnp.dot(e.astype(BF16), v, preferred_element_type=F32) / den


def _seg_of_tile(i, n_lat_tiles, tiles_per_batch, n_batch):
    return jnp.where(i < n_lat_tiles, i // tiles_per_batch, n_batch)


def _proj_attn_kernel(x_ref, mod_ref, cos_ref, sin_ref, w_ref, q_ref, k_ref, v_ref):
    h = _modulated(x_ref, mod_ref, 0).astype(BF16)
    cos = cos_ref[...]
    sin = sin_ref[...]
    lane = lax.broadcasted_iota(jnp.int32, cos.shape, 1)
    low_half = (lane % 32) < 16

    def rope(a):
        partner = jnp.where(low_half, pltpu.roll(a, LANES - 16, 1), pltpu.roll(a, 16, 1))
        return a * cos + partner * sin

    def rope_cols(c0, width):
        a = jnp.dot(h, w_ref[:, c0:c0 + width], preferred_element_type=F32)
        return jnp.concatenate([rope(a[:, j * LANES:(j + 1) * LANES]) for j in range(width // LANES)], axis=1)

    nq = q_ref.shape[1]
    nk = k_ref.shape[1]
    wide = 2 * LANES
    for j in range(nq // wide):
        q_ref[:, j * wide:(j + 1) * wide] = (rope_cols(j * wide, wide) * (HEAD_DIM ** -0.5)).astype(BF16)
    for j in range(nk // wide):
        k_ref[:, j * wide:(j + 1) * wide] = rope_cols(nq + j * wide, wide).astype(BF16)
    v_ref[...] = jnp.dot(h, w_ref[:, nq + nk:], preferred_element_type=F32).astype(BF16)


def _proj_nat_kernel(x_ref, mod_ref, w_ref, q_ref, k_ref, v_ref):
    h = _modulated(x_ref, mod_ref, 0).astype(BF16)
    d = q_ref.shape[1]
    q_ref[...] = (jnp.dot(h, w_ref[:, :d], preferred_element_type=F32) * (HEAD_DIM ** -0.5)).astype(BF16)
    k_ref[...] = jnp.dot(h, w_ref[:, d:2 * d], preferred_element_type=F32).astype(BF16)
    v_ref[...] = jnp.dot(h, w_ref[:, 2 * d:], preferred_element_type=F32).astype(BF16)


def _proj_conv_kernel(x_ref, mod_ref, w_ref, gb_ref, z_ref):
    h = _modulated(x_ref, mod_ref, 0).astype(BF16)
    d = gb_ref.shape[1]
    gb_ref[...] = jnp.dot(h, w_ref[:, :d], preferred_element_type=F32).astype(BF16)
    gc = jnp.dot(h, w_ref[:, d:2 * d], preferred_element_type=F32)
    u = jnp.dot(h, w_ref[:, 2 * d:], preferred_element_type=F32)
    z_ref[...] = (gc * u).astype(BF16)


def _proj_in(kind, xs, mods, w_bf, dims, rope_tabs=None):
    n_batch, seq, ctx_len = dims
    t, d = xs.shape
    tm = TOKEN_TILE
    n_lat_tiles = n_batch * seq // tm
    tpb = seq // tm
    seg = functools.partial(_seg_of_tile, n_lat_tiles=n_lat_tiles, tiles_per_batch=tpb, n_batch=n_batch)
    x_spec = pl.BlockSpec((tm, d), lambda i: (i, 0))
    mod_spec = pl.BlockSpec((None, 6, d), lambda i: (seg(i), 0, 0))
    w_spec = pl.BlockSpec(w_bf.shape, lambda i: (0, 0))

    def tok_out(n):
        return jax.ShapeDtypeStruct((t, n), BF16), pl.BlockSpec((tm, n), lambda i: (i, 0))

    if kind == 0:
        cos, sin = rope_tabs
        tab_spec = pl.BlockSpec((tm, LANES), lambda i: (jnp.where(i < n_lat_tiles, i % tpb, tpb), 0))
        outs = [tok_out(d), tok_out(A_KV), tok_out(A_KV)]
        body, ins, in_specs = _proj_attn_kernel, (xs, mods, cos, sin, w_bf), [x_spec, mod_spec, tab_spec, tab_spec, w_spec]
    elif kind == 1:
        outs = [tok_out(d), tok_out(d)]
        body, ins, in_specs = _proj_conv_kernel, (xs, mods, w_bf), [x_spec, mod_spec, w_spec]
    else:
        outs = [tok_out(d), tok_out(d), tok_out(d)]
        body, ins, in_specs = _proj_nat_kernel, (xs, mods, w_bf), [x_spec, mod_spec, w_spec]
    return pl.pallas_call(
        body,
        out_shape=[o[0] for o in outs],
        grid=(t // tm,),
        in_specs=in_specs,
        out_specs=[o[1] for o in outs],
        compiler_params=_cparams(1),
        name=f"proj_in_{kind}",
    )(*ins)


def _rope_tables(seq, tm):
    quarter = HEAD_DIM // 4
    inv_freq = ROPE_BASE ** (-jnp.arange(quarter, dtype=F32) / quarter)
    tpos = jnp.arange(seq)
    ang_r = (tpos // GRID_W).astype(F32)[:, None] * inv_freq
    ang_c = (tpos % GRID_W).astype(F32)[:, None] * inv_freq
    cos_h = jnp.concatenate([jnp.cos(ang_r), jnp.cos(ang_r), jnp.cos(ang_c), jnp.cos(ang_c)], axis=1)
    sin_h = jnp.concatenate([-jnp.sin(ang_r), jnp.sin(ang_r), -jnp.sin(ang_c), jnp.sin(ang_c)], axis=1)
    reps = LANES // HEAD_DIM
    cos = jnp.concatenate([jnp.tile(cos_h, (1, reps)), jnp.ones((tm, LANES), F32)], axis=0)
    sin = jnp.concatenate([jnp.tile(sin_h, (1, reps)), jnp.zeros((tm, LANES), F32)], axis=0)
    return cos, sin


def _window_attn_kernel(sink_ref, q_ref, kp_ref, kc_ref, kn_ref, kx_ref, vp_ref, vc_ref, vn_ref, vx_ref, o_ref, *, seq):
    qi = pl.program_id(1)
    blk = A_BLOCK
    group = N_HEADS // A_KV_HEADS
    kcat = jnp.concatenate([kp_ref[...], kc_ref[...], kn_ref[...], kx_ref[...]], axis=0)
    vcat = jnp.concatenate([vp_ref[...], vc_ref[...], vn_ref[...], vx_ref[...]], axis=0)
    nkeys = kcat.shape[0]
    rows = group * blk
    row = lax.broadcasted_iota(jnp.int32, (rows, nkeys), 0) & (blk - 1)
    col = lax.broadcasted_iota(jnp.int32, (rows, nkeys), 1)
    kpos = qi * blk - blk + col
    rel = col - blk - row
    local_ok = (rel >= -A_WINDOW) & (rel <= A_WINDOW) & (kpos >= 0) & (kpos < seq)
    valid = local_ok | (col >= 3 * blk)
    for kk in range(A_KV_HEADS):
        k_h = kcat[:, kk * HEAD_DIM:(kk + 1) * HEAD_DIM]
        v_h = vcat[:, kk * HEAD_DIM:(kk + 1) * HEAD_DIM]
        heads = [kk * group + g for g in range(group)]
        q_st = jnp.concatenate([q_ref[:, h * HEAD_DIM:(h + 1) * HEAD_DIM] for h in heads], axis=0)
        sink_col = jnp.concatenate([jnp.full((blk, 1), sink_ref[h], F32) for h in heads], axis=0)
        s = jnp.where(valid, _dot_nt(q_st, k_h), NEG_INF)
        o = _softmax_pv(s, v_h, sink_col).astype(o_ref.dtype)
        for g, h in enumerate(heads):
            o_ref[:, h * HEAD_DIM:(h + 1) * HEAD_DIM] = o[g * blk:(g + 1) * blk, :]


def _window_attn(q, k, v, sink, dims):
    n_batch, seq, ctx_len = dims
    t, d = q.shape
    blk = A_BLOCK
    nb = seq // blk
    t_lat = n_batch * seq
    ctx_blk0 = t_lat // ctx_len

    def q_map(b, i, s):
        return (b * nb + i, 0)

    def kv_map(off):
        return lambda b, i, s: (b * nb + jnp.clip(i + off, 0, nb - 1), 0)

    def ctx_map(b, i, s):
        return (ctx_blk0 + b, 0)

    kv_specs = [pl.BlockSpec((blk, A_KV), kv_map(-1)), pl.BlockSpec((blk, A_KV), kv_map(0)),
                pl.BlockSpec((blk, A_KV), kv_map(1)), pl.BlockSpec((ctx_len, A_KV), ctx_map)]
    return pl.pallas_call(
        functools.partial(_window_attn_kernel, seq=seq),
        out_shape=jax.ShapeDtypeStruct((t_lat, d), BF16),
        grid_spec=pltpu.PrefetchScalarGridSpec(
            num_scalar_prefetch=1,
            grid=(n_batch, nb),
            in_specs=[pl.BlockSpec((blk, d), q_map)] + kv_specs + kv_specs,
            out_specs=pl.BlockSpec((blk, d), q_map),
        ),
        compiler_params=_cparams(2),
        name="window_attn",
    )(sink, q, k, k, k, k, v, v, v, v)


def _ctx_attn_kernel(sink_ref, q_ref, k_ref, v_ref, o_ref, *, n_kv, use_sink):
    group = N_HEADS // n_kv
    n = q_ref.shape[0]
    for kk in range(n_kv):
        k_h = k_ref[:, kk * HEAD_DIM:(kk + 1) * HEAD_DIM]
        v_h = v_ref[:, kk * HEAD_DIM:(kk + 1) * HEAD_DIM]
        heads = [kk * group + g for g in range(group)]
        q_st = jnp.concatenate([q_ref[:, h * HEAD_DIM:(h + 1) * HEAD_DIM] for h in heads], axis=0)
        sink_col = None
        if use_sink:
            sink_col = jnp.concatenate([jnp.full((n, 1), sink_ref[h], F32) for h in heads], axis=0)
        o = _softmax_pv(_dot_nt(q_st, k_h), v_h, sink_col).astype(o_ref.dtype)
        for g, h in enumerate(heads):
            o_ref[:, h * HEAD_DIM:(h + 1) * HEAD_DIM] = o[g * n:(g + 1) * n, :]


def _ctx_attn(q, k, v, sink, dims, n_kv, use_sink):
    n_batch, seq, ctx_len = dims
    d = q.shape[1]
    kvw = n_kv * HEAD_DIM
    blk0 = n_batch * seq // ctx_len
    return pl.pallas_call(
        functools.partial(_ctx_attn_kernel, n_kv=n_kv, use_sink=use_sink),
        out_shape=jax.ShapeDtypeStruct((n_batch * ctx_len, d), BF16),
        grid_spec=pltpu.PrefetchScalarGridSpec(
            num_scalar_prefetch=1,
            grid=(n_batch,),
            in_specs=[pl.BlockSpec((ctx_len, d), lambda b, s: (blk0 + b, 0)),
                      pl.BlockSpec((ctx_len, kvw), lambda b, s: (blk0 + b, 0)),
                      pl.BlockSpec((ctx_len, kvw), lambda b, s: (blk0 + b, 0))],
            out_specs=pl.BlockSpec((ctx_len, d), lambda b, s: (b, 0)),
        ),
        compiler_params=_cparams(1),
        name="ctx_attn",
    )(sink, q, k, v)


def _nat_kernel(q_ref, k0_ref, k1_ref, k2_ref, kx_ref, v0_ref, v1_ref, v2_ref, vx_ref, bias_ref, o_ref):
    nq = q_ref.shape[0]
    low = lax.broadcasted_iota(jnp.int32, (nq, LANES), 1) < HEAD_DIM
    for a in range(N_HEADS // 2):
        sl = slice(a * LANES, (a + 1) * LANES)
        qp = q_ref[:, sl]
        k_loc = jnp.concatenate([k0_ref[:, sl], k1_ref[:, sl], k2_ref[:, sl]], axis=0)
        k_ctx = kx_ref[:, sl]
        v_all = jnp.concatenate([v0_ref[:, sl], v1_ref[:, sl], v2_ref[:, sl], vx_ref[:, sl]], axis=0)
        outs = []
        for hh in range(2):
            qm = jnp.where(low if hh == 0 else jnp.logical_not(low), qp, jnp.zeros_like(qp))
            s = jnp.concatenate([_dot_nt(qm, k_loc) + bias_ref[2 * a + hh], _dot_nt(qm, k_ctx)], axis=-1)
            outs.append(_softmax_pv(s, v_all))
        o_ref[:, sl] = jnp.where(low, outs[0], outs[1]).astype(o_ref.dtype)


def _nat_bias_table(rpb, kh):
    g, wr = NAT_ROWS, NAT_WINDOW_ROWS
    qc = jnp.arange(GRID_W)
    kc = jnp.arange(GRID_W)
    q_start = jnp.clip(qc - NA_KW // 2, 0, GRID_W - NA_KW)
    col_ok = (kc[None, :] >= q_start[:, None]) & (kc[None, :] < q_start[:, None] + NA_KW)
    dc = jnp.clip(kc[None, :] - qc[:, None] + NA_KW - 1, 0, 2 * NA_KW - 2)
    j = jnp.arange(g)
    i = jnp.arange(wr)
    dc_hot = (dc[:, :, None] == jnp.arange(2 * NA_KW - 1)).astype(F32)
    by_col = jnp.einsum('hac,qkc->haqk', rpb.astype(F32), dc_hot, precision=HIGHEST)
    tables = []
    for q_rel, rs_rel in ((j, 0 * j), (g + j, j), (wr - g + j, 0 * j + wr - kh)):
        row_ok = (i[None, :] >= rs_rel[:, None]) & (i[None, :] < rs_rel[:, None] + kh)
        dr = jnp.clip(i[None, :] - q_rel[:, None] + NA_KH_MAX - 1, 0, 2 * NA_KH_MAX - 2)
        dr_hot = (dr[:, :, None] == jnp.arange(2 * NA_KH_MAX - 1)).astype(F32)
        b = jnp.einsum('jia,haqk->hjiqk', dr_hot, by_col, precision=HIGHEST)
        b = jnp.where(row_ok[None, :, :, None, None] & col_ok[None, None, None], b, NEG_INF)
        tables.append(b.transpose(0, 1, 3, 2, 4).reshape(N_HEADS, g * GRID_W, wr * GRID_W))
    return jnp.stack(tables)


def _nat_attn(q, k, v, rpb, dims):
    n_batch, seq, ctx_len = dims
    t, d = q.shape
    n_rows = seq // GRID_W
    kh = min(NA_KH_MAX, n_rows)
    g, wr = NAT_ROWS, NAT_WINDOW_ROWS
    n_groups = n_rows // g
    assert kh == NA_KH_MAX and wr == 3 * g and wr >= kh + g - 1 and n_rows % g == 0 and n_groups >= 3
    bias = _nat_bias_table(rpb, kh)
    blk = g * GRID_W
    bpb = seq // blk
    ctx_blk0 = n_batch * seq // ctx_len

    def case_of(m):
        return jnp.where(m == 0, 0, jnp.where(m == n_groups - 1, 2, 1))

    def win(off):
        return lambda m, b: (b * bpb + jnp.clip(m - 1, 0, n_groups - 3) + off, 0)

    tok = pl.BlockSpec((blk, d), lambda m, b: (b * bpb + m, 0))
    ctx = pl.BlockSpec((ctx_len, d), lambda m, b: (ctx_blk0 + b, 0))
    kv = [pl.BlockSpec((blk, d), win(0)), pl.BlockSpec((blk, d), win(1)), pl.BlockSpec((blk, d), win(2)), ctx]
    return pl.pallas_call(
        _nat_kernel,
        out_shape=jax.ShapeDtypeStruct((n_batch * seq, d), BF16),
        grid=(n_groups, n_batch),
        in_specs=[tok] + kv + kv + [
            pl.BlockSpec((None, N_HEADS, blk, wr * GRID_W), lambda m, b: (case_of(m), 0, 0, 0))],
        out_specs=tok,
        compiler_params=_cparams(2),
        name="nat_attn",
    )(q, k, k, k, k, v, v, v, v, bias)


def _bf16_part(a):
    return lax.bitcast_convert_type(lax.bitcast_convert_type(a, jnp.uint32) & jnp.uint32(0xFFFF0000), F32)


def _route_tile(h2, wr_ref, br_ref, rc):
    mt = h2.shape[0]
    h_top = _bf16_part(h2)
    both = _dot_nt(wr_ref[...], h_top.astype(BF16)) + _dot_nt(wr_ref[...], (h2 - h_top).astype(BF16))
    logits = (both[:LANES] + both[LANES:] + br_ref[...])[:ROUTE_ROWS]
    row_i = lax.broadcasted_iota(jnp.int32, logits.shape, 0)
    row = row_i.astype(F32)
    big = jnp.float32(1e9)
    low = jnp.float32(-3e38)
    is_group = row_i < N_GROUPS
    gl = jnp.where(is_group, logits, low)
    gmax = jnp.max(gl, axis=0, keepdims=True)
    g_idx = jnp.min(jnp.where(gl == gmax, row, big), axis=0, keepdims=True)
    gsum = jnp.sum(jnp.where(is_group, jnp.exp(gl - gmax), 0.0), axis=0, keepdims=True)
    g_prob = 1.0 / gsum
    e_lo = N_GROUPS + g_idx * EXPERTS_PER_GROUP
    in_group = (row >= e_lo) & (row < e_lo + EXPERTS_PER_GROUP)
    el = jnp.where(in_group, logits, low)
    v1 = jnp.max(el, axis=0, keepdims=True)
    i1 = jnp.min(jnp.where(el == v1, row, big), axis=0, keepdims=True)
    el2 = jnp.where(row == i1, low, el)
    v2 = jnp.max(el2, axis=0, keepdims=True)
    i2 = jnp.min(jnp.where(el2 == v2, row, big), axis=0, keepdims=True)
    t = jnp.exp(v2 - v1)
    g0 = 1.0 / (1.0 + t) * g_prob
    g1 = t / (1.0 + t) * g_prob

    hit0 = row == i1
    hit1 = row == i2
    onehot = jnp.where(hit0, 1.0, jnp.where(hit1, 1.0, 0.0))
    r_i = lax.broadcasted_iota(jnp.int32, (mt, mt), 0)
    c_i = lax.broadcasted_iota(jnp.int32, (mt, mt), 1)
    earlier = jnp.where(r_i < c_i, 1.0, 0.0).astype(BF16)
    before = jnp.dot(onehot.astype(BF16), earlier, preferred_element_type=F32)
    cnt = jnp.sum(onehot, axis=1, keepdims=True)
    padded = jnp.floor((cnt + (CHUNK - 1)) * (1.0 / CHUNK)) * CHUNK
    a_i = lax.broadcasted_iota(jnp.int32, (ROUTE_ROWS, ROUTE_ROWS), 0)
    b_i = lax.broadcasted_iota(jnp.int32, (ROUTE_ROWS, ROUTE_ROWS), 1)
    lower = jnp.where(b_i < a_i, 1.0, 0.0)
    goff = jnp.dot(lower, jnp.broadcast_to(padded, (ROUTE_ROWS, LANES)), precision=HIGHEST,
                   preferred_element_type=F32)[:, 0:1]
    pos = before + goff
    slot0 = jnp.sum(jnp.where(hit0, pos, 0.0), axis=0, keepdims=True)
    slot1 = jnp.sum(jnp.where(hit1, pos, 0.0), axis=0, keepdims=True)
    return (i1 - N_GROUPS, i2 - N_GROUPS, g0, g1, slot0, slot1), cnt, goff


def _post_mixer(o_bf, x_ref, mod_ref, g_ref, b_ref, wo_ref, wr_ref, br_ref, x1_ref, xs_ref, route_ref, meta_ref, alpha):
    y = jnp.dot(o_bf, wo_ref[...], preferred_element_type=F32)
    x1 = _layer_norm(alpha * x_ref[...] + mod_ref[2:3, :] * y, g_ref, b_ref)
    x1_ref[...] = x1
    h2 = x1 * (1.0 + mod_ref[4:5, :]) + mod_ref[3:4, :]
    rc = xs_ref.shape[0]
    mt = h2.shape[0]
    rows, cnt, goff = _route_tile(h2, wr_ref, br_ref, rc)

    s_i = lax.broadcasted_iota(jnp.int32, (rc, mt), 0).astype(F32)
    pick = jnp.where(s_i == rows[4], 1.0, jnp.where(s_i == rows[5], 1.0, 0.0)).astype(BF16)
    xs_ref[...] = jnp.dot(pick, h2.astype(BF16), preferred_element_type=F32)

    sub = lax.broadcasted_iota(jnp.int32, (LANES, mt), 0)
    stacked = jnp.zeros((LANES, mt), F32)
    for j, r in enumerate(rows):
        stacked = jnp.where(sub == j, r, stacked)
    route_ref[...] = stacked.T[:, :ROUTE_W]
    lane = lax.broadcasted_iota(jnp.int32, (ROUTE_ROWS, LANES), 1)
    meta_ref[...] = jnp.where(lane == 0, cnt, jnp.where(lane == 1, goff, 0.0))


def _post_attn_kernel(ol_ref, oc_ref, x_ref, mod_ref, g_ref, b_ref, wo_ref, wr_ref, br_ref, x1_ref, xs_ref, route_ref, meta_ref,
                      *, alpha, n_lat_tiles, n_tiles):
    is_lat = jnp.minimum(pl.program_id(0), n_tiles - 1) < n_lat_tiles
    o = jnp.where(is_lat, ol_ref[...], oc_ref[...])
    _post_mixer(o, x_ref, mod_ref, g_ref, b_ref, wo_ref, wr_ref, br_ref, x1_ref, xs_ref, route_ref, meta_ref, alpha)


def _post_conv_kernel(gb_ref, z_ref, zp_ref, zn_ref, cw_ref, x_ref, mod_ref, g_ref, b_ref, wo_ref, wr_ref, br_ref,
                      x1_ref, xs_ref, route_ref, meta_ref, *, alpha, t_lat, seq, ctx_len, n_tiles):
    i = jnp.minimum(pl.program_id(0), n_tiles - 1)
    tm, d = z_ref.shape
    halo = zp_ref.shape[0]
    z = z_ref[...].astype(F32)
    row = lax.broadcasted_iota(jnp.int32, (tm, 1), 0)
    g = i * tm + row
    pos = jnp.where(g < t_lat, g % seq, g % ctx_len)
    n_seq = jnp.where(g < t_lat, seq, ctx_len)
    prev_row = zp_ref[halo - 1:halo, :].astype(F32)
    next_row = zn_ref[0:1, :].astype(F32)
    z_prev = jnp.where(row == 0, prev_row, pltpu.roll(z, 1, 0))
    z_next = jnp.where(row == tm - 1, next_row, pltpu.roll(z, tm - 1, 0))
    z_prev = jnp.where(pos == 0, 0.0, z_prev)
    z_next = jnp.where(pos == n_seq - 1, 0.0, z_next)
    y = cw_ref[0:1, :] * z_prev + cw_ref[1:2, :] * z + cw_ref[2:3, :] * z_next
    o = (gb_ref[...].astype(F32) * y).astype(BF16)
    _post_mixer(o, x_ref, mod_ref, g_ref, b_ref, wo_ref, wr_ref, br_ref, x1_ref, xs_ref, route_ref, meta_ref, alpha)


def _compact_rows(mt):
    return TOP_K * mt + N_EXPERTS * CHUNK


def _spare_tiles(rc, bm):
    return -(-2 * bm // rc)


def _post_mixer_call(kind, mixer_out, xs, mods, ln_g, ln_b, wo_bf, wr, br, dims, n_tok, alpha, conv_w=None):
    n_batch, seq, ctx_len = dims
    d = xs.shape[1]
    tm = MOE_TILE
    rc = _compact_rows(tm)
    n_tiles = n_tok // tm
    n_lat_tiles = n_batch * seq // tm
    tpb = seq // tm
    seg = functools.partial(_seg_of_tile, n_lat_tiles=n_lat_tiles, tiles_per_batch=tpb, n_batch=n_batch)
    n_spare = _spare_tiles(rc, EXPERT_BLOCK)
    tile = lambda i: jnp.minimum(i, n_tiles - 1)
    tok = pl.BlockSpec((tm, d), lambda i: (tile(i), 0))
    full = lambda a: pl.BlockSpec(a.shape, lambda i: (0,) * a.ndim)
    common_ins = (xs, mods, ln_g, ln_b, wo_bf, wr, br)
    common_specs = [tok, pl.BlockSpec((None, 6, d), lambda i: (seg(tile(i)), 0, 0)), full(ln_g), full(ln_b),
                    full(wo_bf), full(wr), full(br)]
    if kind == 1:
        gb, z = mixer_out
        halo = 16
        hb = tm // halo
        last_blk = z.shape[0] // halo - 1
        ins = (gb, z, z, z, conv_w) + common_ins
        in_specs = [tok, tok,
                    pl.BlockSpec((halo, d), lambda i: (jnp.maximum(tile(i) * hb - 1, 0), 0)),
                    pl.BlockSpec((halo, d), lambda i: (jnp.minimum((tile(i) + 1) * hb, last_blk), 0)),
                    full(conv_w)] + common_specs
        body = functools.partial(_post_conv_kernel, alpha=alpha, t_lat=n_batch * seq, seq=seq, ctx_len=ctx_len,
                                 n_tiles=n_tiles)
    else:
        o_lat, o_ctx = mixer_out
        n_ctx_tiles = o_ctx.shape[0] // tm
        ins = (o_lat, o_ctx) + common_ins
        in_specs = [pl.BlockSpec((tm, d), lambda i: (jnp.minimum(tile(i), n_lat_tiles - 1), 0)),
                    pl.BlockSpec((tm, d), lambda i: (jnp.clip(tile(i) - n_lat_tiles, 0, n_ctx_tiles - 1), 0))] + common_specs
        body = functools.partial(_post_attn_kernel, alpha=alpha, n_lat_tiles=n_lat_tiles, n_tiles=n_tiles)
    return pl.pallas_call(
        body,
        out_shape=[jax.ShapeDtypeStruct((n_tok, d), F32),
                   jax.ShapeDtypeStruct(((n_tiles + n_spare) * rc, d), F32),
                   jax.ShapeDtypeStruct((n_tok, ROUTE_W), F32),
                   jax.ShapeDtypeStruct((n_tiles, ROUTE_ROWS, LANES), F32)],
        grid=(n_tiles + n_spare,),
        in_specs=in_specs,
        out_specs=[tok, pl.BlockSpec((rc, d), lambda i: (i, 0)),
                   pl.BlockSpec((tm, ROUTE_W), lambda i: (tile(i), 0)),
                   pl.BlockSpec((None, ROUTE_ROWS, LANES), lambda i: (tile(i), 0, 0))],
        compiler_params=_cparams(1),
        name=f"post_mixer_{kind}",
    )(*ins)


def _expert_schedule(meta, rc, bm):
    n_tiles = meta.shape[0]
    cpb = bm // CHUNK
    cnt = meta[:, N_GROUPS:N_GROUPS + N_EXPERTS, 0].astype(jnp.int32)
    goff = meta[:, N_GROUPS:N_GROUPS + N_EXPERTS, 1].astype(jnp.int32)
    nch = ((cnt + CHUNK - 1) // CHUNK).T
    incl = jnp.cumsum(nch, axis=1)
    per_e = incl[:, -1]
    nbk = (per_e + cpb - 1) // cpb
    blk_start = jnp.cumsum(nbk) - nbk
    src0 = jnp.arange(n_tiles, dtype=jnp.int32)[None, :] * rc + goff.T
    n_blocks = (TOP_K * n_tiles * MOE_TILE + (CHUNK - 1) * N_EXPERTS * n_tiles) // CHUNK // cpb + N_EXPERTS
    blk = jnp.arange(n_blocks, dtype=jnp.int32)
    blk_e = jnp.clip(jnp.sum(blk_start[None, :] <= blk[:, None], axis=1) - 1, 0, N_EXPERTS - 1).astype(jnp.int32)
    onehot = (blk_e[:, None] == jnp.arange(N_EXPERTS)[None, :]).astype(F32)
    tables = jnp.concatenate([incl, incl - nch, src0, blk_start[:, None], per_e[:, None]], axis=1).astype(F32)
    per_blk = jnp.round(jnp.dot(onehot, tables, precision=HIGHEST)).astype(jnp.int32)
    incl_b, excl_b, src_b = (per_blk[:, None, i * n_tiles:(i + 1) * n_tiles] for i in range(3))
    start_b, total_b = per_blk[:, 3 * n_tiles], per_blk[:, 3 * n_tiles + 1]
    lq = ((blk - start_b) * cpb)[:, None] + jnp.arange(cpb, dtype=jnp.int32)[None, :]
    run = jnp.sum(incl_b <= lq[:, :, None], axis=2)
    pick = jnp.arange(n_tiles, dtype=jnp.int32)[None, None, :] == run[:, :, None]
    first = jnp.sum(jnp.where(pick, excl_b, 0), axis=2)
    src = jnp.sum(jnp.where(pick, src_b, 0), axis=2)
    chunk_src = jnp.where(lq < total_b[:, None], src + CHUNK * (lq - first), -1).astype(jnp.int32).reshape(-1)
    n_used = jnp.sum(nbk).astype(jnp.int32).reshape(1)
    return chunk_src, blk_e, n_used, n_blocks


def _expert_kernel(blk_e_ref, src_ref, n_used_ref, x_hbm, wg_ref, wu_ref, wd_ref, y_hbm, xbuf, ybuf, gsem, ssem,
                   wgu_bf, wd_bf, *, zero_row, trash_row):
    b = pl.program_id(0)
    nb = pl.num_programs(0)
    _, bm, d = xbuf.shape
    ff = wg_ref.shape[1]
    cpb = bm // CHUNK
    n_used = n_used_ref[0]

    def gather(blk, slot, real):
        base = jnp.where(real, blk, 0) * cpb
        for j in range(cpb):
            v = jnp.where(real, src_ref[base + j], -1)
            row = pl.multiple_of(jnp.where(v < 0, zero_row, v), CHUNK)
            pltpu.make_async_copy(x_hbm.at[pl.ds(row, CHUNK), :], xbuf.at[slot, pl.ds(j * CHUNK, CHUNK), :],
                                  gsem.at[slot]).start()

    def scatter(blk, slot, real):
        base = jnp.where(real, blk, 0) * cpb
        for j in range(cpb):
            v = jnp.where(real, src_ref[base + j], -1)
            row = pl.multiple_of(jnp.where(v < 0, trash_row + slot * bm + j * CHUNK, v), CHUNK)
            pltpu.make_async_copy(ybuf.at[slot, pl.ds(j * CHUNK, CHUNK), :], y_hbm.at[pl.ds(row, CHUNK), :],
                                  ssem.at[slot]).start()

    def wait_gather(slot):
        pltpu.make_async_copy(x_hbm.at[pl.ds(0, bm), :], xbuf.at[slot], gsem.at[slot]).wait()

    def wait_scatter(slot):
        pltpu.make_async_copy(ybuf.at[slot], y_hbm.at[pl.ds(0, bm), :], ssem.at[slot]).wait()

    in_use = b < n_used

    @pl.when((b == 0) & in_use)
    def _():
        ybuf[...] = jnp.zeros_like(ybuf)
        scatter(0, 0, False)
        scatter(0, 1, False)
        gather(0, 0, True)

    @pl.when(in_use & ((b == 0) | (blk_e_ref[b] != blk_e_ref[jnp.maximum(b - 1, 0)])))
    def _():
        wgu_bf[:, :ff] = wg_ref[...].astype(BF16)
        wgu_bf[:, ff:] = wu_ref[...].astype(BF16)
        wd_bf[...] = wd_ref[...].astype(BF16)

    @pl.when(in_use)
    def _():
        slot = b % 2
        wait_gather(slot)
        wait_scatter(slot)
        gather(b + 1, 1 - slot, b + 1 < n_used)
        for part in range(EXPERT_PARTS):
            rows = pl.ds(part * (bm // EXPERT_PARTS), bm // EXPERT_PARTS)
            gu = jnp.dot(xbuf[slot, rows, :].astype(BF16), wgu_bf[...], preferred_element_type=F32)
            gate = gu[:, :ff]
            act = gate / (1.0 + jnp.exp(-gate)) * gu[:, ff:]
            ybuf[slot, rows, :] = jnp.dot(act.astype(BF16), wd_bf[...], preferred_element_type=F32)
        scatter(b, slot, True)

    @pl.when((b == nb - 1) & (n_used > 0))
    def _():
        wait_gather(n_used % 2)
        wait_scatter(0)
        wait_scatter(1)


def _experts(xs_c, chunk_src, blk_e, n_used, n_blocks, w_gate, w_up, w_down, layer, rc, bm):
    d = xs_c.shape[1]
    ff = w_gate.shape[-1]
    n_tiles = xs_c.shape[0] // rc - _spare_tiles(rc, bm)
    assert rc - TRASH_ROWS >= TOP_K * MOE_TILE + N_EXPERTS * (CHUNK - 1) and bm % (EXPERT_PARTS * CHUNK) == 0
    return pl.pallas_call(
        functools.partial(_expert_kernel, zero_row=rc - TRASH_ROWS, trash_row=n_tiles * rc),
        out_shape=jax.ShapeDtypeStruct(xs_c.shape, xs_c.dtype),
        grid_spec=pltpu.PrefetchScalarGridSpec(
            num_scalar_prefetch=3,
            grid=(n_blocks,),
            in_specs=[
                pl.BlockSpec(memory_space=pl.ANY),
                pl.BlockSpec((None, None, d, ff), lambda b, be, cs, nu: (layer, be[b], 0, 0)),
                pl.BlockSpec((None, None, d, ff), lambda b, be, cs, nu: (layer, be[b], 0, 0)),
                pl.BlockSpec((None, None, ff, d), lambda b, be, cs, nu: (layer, be[b], 0, 0)),
            ],
            out_specs=pl.BlockSpec(memory_space=pl.ANY),
            scratch_shapes=[
                pltpu.VMEM((2, bm, d), F32),
                pltpu.VMEM((2, bm, d), F32),
                pltpu.SemaphoreType.DMA((2,)),
                pltpu.SemaphoreType.DMA((2,)),
                pltpu.VMEM((d, 2 * ff), BF16),
                pltpu.VMEM((ff, d), BF16),
            ],
        ),
        input_output_aliases={3: 0},
        compiler_params=_cparams(1),
        name="moe_experts",
    )(blk_e, chunk_src, n_used, xs_c, w_gate, w_up, w_down)


def _combine_kernel(ys_ref, x1_ref, route_ref, mod_ref, g_ref, b_ref, o_ref, *, alpha):
    rc = ys_ref.shape[0]
    mt = x1_ref.shape[0]
    yc = ys_ref[...].astype(BF16)
    route = route_ref[...]
    s_i = lax.broadcasted_iota(jnp.int32, (mt, rc), 1).astype(F32)
    y = None
    for k in range(TOP_K):
        pick = jnp.where(s_i == route[:, 4 + k:5 + k], 1.0, 0.0).astype(BF16)
        yk = route[:, 2 + k:3 + k] * jnp.dot(pick, yc, preferred_element_type=F32)
        y = yk if y is None else y + yk
    o_ref[...] = _layer_norm(alpha * x1_ref[...] + mod_ref[5:6, :] * y, g_ref, b_ref)


def _combine(ys_c, x1, route, mods, ln_g, ln_b, dims, n_tok, alpha, rc):
    n_batch, seq, ctx_len = dims
    d = x1.shape[1]
    tg = MOE_TILE
    n_lat_tiles = n_batch * seq // tg
    tpb = seq // tg
    seg = functools.partial(_seg_of_tile, n_lat_tiles=n_lat_tiles, tiles_per_batch=tpb, n_batch=n_batch)
    return pl.pallas_call(
        functools.partial(_combine_kernel, alpha=alpha),
        out_shape=jax.ShapeDtypeStruct((n_tok, d), F32),
        grid=(n_tok // tg,),
        in_specs=[
            pl.BlockSpec((rc, d), lambda i: (i, 0)),
            pl.BlockSpec((tg, d), lambda i: (i, 0)),
            pl.BlockSpec((tg, ROUTE_W), lambda i: (i, 0)),
            pl.BlockSpec((None, 6, d), lambda i: (seg(i), 0, 0)),
            pl.BlockSpec(ln_g.shape, lambda i: (0, 0)),
            pl.BlockSpec(ln_b.shape, lambda i: (0, 0)),
        ],
        out_specs=pl.BlockSpec((tg, d), lambda i: (i, 0)),
        compiler_params=_cparams(1),
        name="moe_combine",
    )(ys_c, x1, route, mods, ln_g, ln_b)


def kernel(x, c, ctx, c_ctx, w_ada, b_ada, ln1_g, ln1_b, ln2_g, ln2_b, attn_w_qkv, attn_w_o, attn_sink, conv_w_in, conv_w,
           conv_w_out, nat_w_qkv, nat_w_o, nat_rpb, router_w_group, router_b_group, router_w_expert, router_b_expert,
           expert_w_gate, expert_w_up, expert_w_down):
    n_batch, seq, d = x.shape
    ctx_len = ctx.shape[1]
    depth = w_ada.shape[0]
    dims = (n_batch, seq, ctx_len)
    t_lat = n_batch * seq
    t_all = t_lat + n_batch * ctx_len
    alpha = float((2 * depth) ** 0.25)
    assert d == D_MODEL and n_batch + 1 <= ADA_ROWS
    assert seq % TOKEN_TILE == 0 and (n_batch * ctx_len) % TOKEN_TILE == 0 and seq % GRID_W == 0
    assert TOKEN_TILE % ctx_len == 0 or ctx_len % TOKEN_TILE == 0
    rc = _compact_rows(MOE_TILE)

    cc = jnp.zeros((ADA_ROWS, d), F32).at[:n_batch].set(c).at[n_batch].set(c_ctx)
    mods_all = _ada(cc, w_ada, b_ada).reshape(depth, ADA_ROWS, 6, d)
    rope_tabs = _rope_tables(seq, TOKEN_TILE)
    xs = jnp.concatenate([x.reshape(t_lat, d), ctx.reshape(n_batch * ctx_len, d)], axis=0)

    n_route_pad = LANES - N_GROUPS - N_EXPERTS
    for i in range(depth):
        last = i == depth - 1
        j = i // 3
        kind = i % 3
        mods = mods_all[i]
        n_tok = t_lat if last else t_all
        g1, b1 = ln1_g[i].reshape(1, d), ln1_b[i].reshape(1, d)
        g2, b2 = ln2_g[i].reshape(1, d), ln2_b[i].reshape(1, d)
        wr = jnp.concatenate([router_w_group[i], router_w_expert[i], jnp.zeros((d, n_route_pad), F32)], axis=1).T
        wr_top = _bf16_part(wr)
        wr = jnp.concatenate([wr_top.astype(BF16), (wr - wr_top).astype(BF16)], axis=0)
        br = jnp.concatenate([router_b_group[i], router_b_expert[i], jnp.zeros((n_route_pad,), F32)]).reshape(LANES, 1)

        if kind == 0:
            q, k, v = _proj_in(0, xs, mods, attn_w_qkv[j].astype(BF16), dims, rope_tabs)
            o = _window_attn(q, k, v, attn_sink[j], dims)
            o_ctx = o if last else _ctx_attn(q, k, v, attn_sink[j], dims, A_KV_HEADS, True)
            post = _post_mixer_call(0, (o, o_ctx), xs, mods, g1, b1, attn_w_o[j].astype(BF16), wr, br, dims, n_tok, alpha)
        elif kind == 1:
            gb, z = _proj_in(1, xs, mods, conv_w_in[j].astype(BF16), dims)
            post = _post_mixer_call(1, (gb, z), xs, mods, g1, b1, conv_w_out[j].astype(BF16), wr, br, dims, n_tok, alpha,
                                    conv_w=conv_w[j])
        else:
            q, k, v = _proj_in(2, xs, mods, nat_w_qkv[j].astype(BF16), dims)
            o = _nat_attn(q, k, v, nat_rpb[j], dims)
            o_ctx = o if last else _ctx_attn(q, k, v, jnp.zeros((N_HEADS,), F32), dims, N_HEADS, False)
            post = _post_mixer_call(2, (o, o_ctx), xs, mods, g1, b1, nat_w_o[j].astype(BF16), wr, br, dims, n_tok, alpha)

        x1, xs_c, route, meta = post
        chunk_src, blk_e, n_used, n_blocks = _expert_schedule(meta, rc, EXPERT_BLOCK)
        ys_c = _experts(xs_c, chunk_src, blk_e, n_used, n_blocks, expert_w_gate, expert_w_up, expert_w_down, i, rc,
                        EXPERT_BLOCK)
        xs = _combine(ys_c, x1, route, mods, g2, b2, dims, n_tok, alpha, rc)
    return xs[:t_lat].reshape(n_batch, seq, d)
```

```python
import functools

import jax
import jax.numpy as jnp
from jax import lax
from jax.experimental import pallas as pl
from jax.experimental.pallas import tpu as pltpu

F32 = jnp.float32
BF16 = jnp.bfloat16
HIGHEST = lax.Precision.HIGHEST

D_MODEL = 1024
HEAD_DIM = 64
N_HEADS = D_MODEL // HEAD_DIM
A_KV_HEADS = N_HEADS // 4
A_KV = A_KV_HEADS * HEAD_DIM
A_WINDOW = 128
A_BLOCK = 128
GRID_W = 64
ROPE_BASE = 10000.0
NA_KH_MAX = 8
NA_KW = 16
N_GROUPS = 4
EXPERTS_PER_GROUP = 8
N_EXPERTS = N_GROUPS * EXPERTS_PER_GROUP
TOP_K = 2
EXPERT_FF = D_MODEL // 2
LN_EPS = 1e-5
NEG_INF = -1e30

LANES = 128
ADA_ROWS = 24
ROUTE_W = 8
TOKEN_TILE = 512
MOE_TILE = 256
EXPERT_BLOCK = 512
EXPERT_PARTS = 2
CHUNK = 8
TRASH_ROWS = 32
ROUTE_ROWS = 64
NAT_ROWS = 4
NAT_WINDOW_ROWS = 12
VMEM_LIMIT = 56 * 1024 * 1024


def _cparams(n_axes=1):
    return pltpu.CompilerParams(dimension_semantics=("arbitrary",) * n_axes, vmem_limit_bytes=VMEM_LIMIT)


def _ada_kernel(c_ref, w_ref, b_ref, o_ref):
    cc = c_ref[...]
    s = cc / (1.0 + jnp.exp(-cc))
    o_ref[...] = jnp.dot(s, w_ref[...], precision=HIGHEST, preferred_element_type=F32) + b_ref[...]


def _ada(cc, w_ada, b_ada):
    depth, d, n = w_ada.shape
    nt = n // 4
    return pl.pallas_call(
        _ada_kernel,
        out_shape=jax.ShapeDtypeStruct((depth, ADA_ROWS, n), F32),
        grid=(depth, n // nt),
        in_specs=[
            pl.BlockSpec((ADA_ROWS, d), lambda i, j: (0, 0)),
            pl.BlockSpec((None, d, nt), lambda i, j: (i, 0, j)),
            pl.BlockSpec((None, 1, nt), lambda i, j: (i, 0, j)),
        ],
        out_specs=pl.BlockSpec((None, ADA_ROWS, nt), lambda i, j: (i, 0, j)),
        compiler_params=_cparams(2),
        name="ada_mod",
    )(cc, w_ada, b_ada.reshape(depth, 1, n))


def _modulated(x_ref, mod_ref, shift_row):
    return x_ref[...] * (1.0 + mod_ref[shift_row + 1:shift_row + 2, :]) + mod_ref[shift_row:shift_row + 1, :]


def _layer_norm(r, g_ref, b_ref):
    mu = jnp.mean(r, axis=-1, keepdims=True)
    rc = r - mu
    var = jnp.mean(rc * rc, axis=-1, keepdims=True)
    return rc * lax.rsqrt(var + LN_EPS) * g_ref[...] + b_ref[...]


def _dot_nt(a, b):
    return lax.dot_general(a, b, (((1,), (1,)), ((), ())), preferred_element_type=F32)


def _softmax_pv(s, v, sink_col=None):
    m = jnp.max(s, axis=-1, keepdims=True)
    if sink_col is not None:
        m = jnp.maximum(m, sink_col)
    e = jnp.exp(s - m)
    den = jnp.sum(e, axis=-1, keepdims=True)
    if sink_col is not None:
        den = den + jnp.exp(sink_col - m)
    return jnp.dot(e.astype(BF16), v, preferred_element_type=F32) / den


def _seg_of_tile(i, n_lat_tiles, tiles_per_batch, n_batch):
    return jnp.where(i < n_lat_tiles, i // tiles_per_batch, n_batch)


def _proj_attn_kernel(x_ref, mod_ref, cos_ref, sin_ref, w_ref, q_ref, k_ref, v_ref):
    h = _modulated(x_ref, mod_ref, 0).astype(BF16)
    cos = cos_ref[...]
    sin = sin_ref[...]
    lane = lax.broadcasted_iota(jnp.int32, cos.shape, 1)
    low_half = (lane % 32) < 16

    def rope(a):
        partner = jnp.where(low_half, pltpu.roll(a, LANES - 16, 1), pltpu.roll(a, 16, 1))
        return a * cos + partner * sin

    def rope_cols(c0, width):
        a = jnp.dot(h, w_ref[:, c0:c0 + width], preferred_element_type=F32)
        return jnp.concatenate([rope(a[:, j * LANES:(j + 1) * LANES]) for j in range(width // LANES)], axis=1)

    nq = q_ref.shape[1]
    nk = k_ref.shape[1]
    wide = 2 * LANES
    for j in range(nq // wide):
        q_ref[:, j * wide:(j + 1) * wide] = (rope_cols(j * wide, wide) * (HEAD_DIM ** -0.5)).astype(BF16)
    for j in range(nk // wide):
        k_ref[:, j * wide:(j + 1) * wide] = rope_cols(nq + j * wide, wide).astype(BF16)
    v_ref[...] = jnp.dot(h, w_ref[:, nq + nk:], preferred_element_type=F32).astype(BF16)


def _proj_nat_kernel(x_ref, mod_ref, w_ref, q_ref, k_ref, v_ref):
    h = _modulated(x_ref, mod_ref, 0).astype(BF16)
    d = q_ref.shape[1]
    q_ref[...] = (jnp.dot(h, w_ref[:, :d], preferred_element_type=F32) * (HEAD_DIM ** -0.5)).astype(BF16)
    k_ref[...] = jnp.dot(h, w_ref[:, d:2 * d], preferred_element_type=F32).astype(BF16)
    v_ref[...] = jnp.dot(h, w_ref[:, 2 * d:], preferred_element_type=F32).astype(BF16)


def _proj_conv_kernel(x_ref, mod_ref, w_ref, gb_ref, z_ref):
    h = _modulated(x_ref, mod_ref, 0).astype(BF16)
    d = gb_ref.shape[1]
    gb_ref[...] = jnp.dot(h, w_ref[:, :d], preferred_element_type=F32).astype(BF16)
    gc = jnp.dot(h, w_ref[:, d:2 * d], preferred_element_type=F32)
    u = jnp.dot(h, w_ref[:, 2 * d:], preferred_element_type=F32)
    z_ref[...] = (gc * u).astype(BF16)


def _proj_in(kind, xs, mods, w_bf, dims, rope_tabs=None):
    n_batch, seq, ctx_len = dims
    t, d = xs.shape
    tm = TOKEN_TILE
    n_lat_tiles = n_batch * seq // tm
    tpb = seq // tm
    seg = functools.partial(_seg_of_tile, n_lat_tiles=n_lat_tiles, tiles_per_batch=tpb, n_batch=n_batch)
    x_spec = pl.BlockSpec((tm, d), lambda i: (i, 0))
    mod_spec = pl.BlockSpec((None, 6, d), lambda i: (seg(i), 0, 0))
    w_spec = pl.BlockSpec(w_bf.shape, lambda i: (0, 0))

    def tok_out(n):
        return jax.ShapeDtypeStruct((t, n), BF16), pl.BlockSpec((tm, n), lambda i: (i, 0))

    if kind == 0:
        cos, sin = rope_tabs
        tab_spec = pl.BlockSpec((tm, LANES), lambda i: (jnp.where(i < n_lat_tiles, i % tpb, tpb), 0))
        outs = [tok_out(d), tok_out(A_KV), tok_out(A_KV)]
        body, ins, in_specs = _proj_attn_kernel, (xs, mods, cos, sin, w_bf), [x_spec, mod_spec, tab_spec, tab_spec, w_spec]
    elif kind == 1:
        outs = [tok_out(d), tok_out(d)]
        body, ins, in_specs = _proj_conv_kernel, (xs, mods, w_bf), [x_spec, mod_spec, w_spec]
    else:
        outs = [tok_out(d), tok_out(d), tok_out(d)]
        body, ins, in_specs = _proj_nat_kernel, (xs, mods, w_bf), [x_spec, mod_spec, w_spec]
    return pl.pallas_call(
        body,
        out_shape=[o[0] for o in outs],
        grid=(t // tm,),
        in_specs=in_specs,
        out_specs=[o[1] for o in outs],
        compiler_params=_cparams(1),
        name=f"proj_in_{kind}",
    )(*ins)


def _rope_tables(seq, tm):
    quarter = HEAD_DIM // 4
    inv_freq = ROPE_BASE ** (-jnp.arange(quarter, dtype=F32) / quarter)
    tpos = jnp.arange(seq)
    ang_r = (tpos // GRID_W).astype(F32)[:, None] * inv_freq
    ang_c = (tpos % GRID_W).astype(F32)[:, None] * inv_freq
    cos_h = jnp.concatenate([jnp.cos(ang_r), jnp.cos(ang_r), jnp.cos(ang_c), jnp.cos(ang_c)], axis=1)
    sin_h = jnp.concatenate([-jnp.sin(ang_r), jnp.sin(ang_r), -jnp.sin(ang_c), jnp.sin(ang_c)], axis=1)
    reps = LANES // HEAD_DIM
    cos = jnp.concatenate([jnp.tile(cos_h, (1, reps)), jnp.ones((tm, LANES), F32)], axis=0)
    sin = jnp.concatenate([jnp.tile(sin_h, (1, reps)), jnp.zeros((tm, LANES), F32)], axis=0)
    return cos, sin


def _window_attn_kernel(sink_ref, q_ref, kp_ref, kc_ref, kn_ref, kx_ref, vp_ref, vc_ref, vn_ref, vx_ref, o_ref, *, seq):
    qi = pl.program_id(1)
    blk = A_BLOCK
    group = N_HEADS // A_KV_HEADS
    kcat = jnp.concatenate([kp_ref[...], kc_ref[...], kn_ref[...], kx_ref[...]], axis=0)
    vcat = jnp.concatenate([vp_ref[...], vc_ref[...], vn_ref[...], vx_ref[...]], axis=0)
    nkeys = kcat.shape[0]
    rows = group * blk
    row = lax.broadcasted_iota(jnp.int32, (rows, nkeys), 0) & (blk - 1)
    col = lax.broadcasted_iota(jnp.int32, (rows, nkeys), 1)
    kpos = qi * blk - blk + col
    rel = col - blk - row
    local_ok = (rel >= -A_WINDOW) & (rel <= A_WINDOW) & (kpos >= 0) & (kpos < seq)
    valid = local_ok | (col >= 3 * blk)
    for kk in range(A_KV_HEADS):
        k_h = kcat[:, kk * HEAD_DIM:(kk + 1) * HEAD_DIM]
        v_h = vcat[:, kk * HEAD_DIM:(kk + 1) * HEAD_DIM]
        heads = [kk * group + g for g in range(group)]
        q_st = jnp.concatenate([q_ref[:, h * HEAD_DIM:(h + 1) * HEAD_DIM] for h in heads], axis=0)
        sink_col = jnp.concatenate([jnp.full((blk, 1), sink_ref[h], F32) for h in heads], axis=0)
        s = jnp.where(valid, _dot_nt(q_st, k_h), NEG_INF)
        o = _softmax_pv(s, v_h, sink_col).astype(o_ref.dtype)
        for g, h in enumerate(heads):
            o_ref[:, h * HEAD_DIM:(h + 1) * HEAD_DIM] = o[g * blk:(g + 1) * blk, :]


def _window_attn(q, k, v, sink, dims):
    n_batch, seq, ctx_len = dims
    t, d = q.shape
    blk = A_BLOCK
    nb = seq // blk
    t_lat = n_batch * seq
    ctx_blk0 = t_lat // ctx_len

    def q_map(b, i, s):
        return (b * nb + i, 0)

    def kv_map(off):
        return lambda b, i, s: (b * nb + jnp.clip(i + off, 0, nb - 1), 0)

    def ctx_map(b, i, s):
        return (ctx_blk0 + b, 0)

    kv_specs = [pl.BlockSpec((blk, A_KV), kv_map(-1)), pl.BlockSpec((blk, A_KV), kv_map(0)),
                pl.BlockSpec((blk, A_KV), kv_map(1)), pl.BlockSpec((ctx_len, A_KV), ctx_map)]
    return pl.pallas_call(
        functools.partial(_window_attn_kernel, seq=seq),
        out_shape=jax.ShapeDtypeStruct((t_lat, d), BF16),
        grid_spec=pltpu.PrefetchScalarGridSpec(
            num_scalar_prefetch=1,
            grid=(n_batch, nb),
            in_specs=[pl.BlockSpec((blk, d), q_map)] + kv_specs + kv_specs,
            out_specs=pl.BlockSpec((blk, d), q_map),
        ),
        compiler_params=_cparams(2),
        name="window_attn",
    )(sink, q, k, k, k, k, v, v, v, v)


def _ctx_attn_kernel(sink_ref, q_ref, k_ref, v_ref, o_ref, *, n_kv, use_sink):
    group = N_HEADS // n_kv
    n = q_ref.shape[0]
    for kk in range(n_kv):
        k_h = k_ref[:, kk * HEAD_DIM:(kk + 1) * HEAD_DIM]
        v_h = v_ref[:, kk * HEAD_DIM:(kk + 1) * HEAD_DIM]
        heads = [kk * group + g for g in range(group)]
        q_st = jnp.concatenate([q_ref[:, h * HEAD_DIM:(h + 1) * HEAD_DIM] for h in heads], axis=0)
        sink_col = None
        if use_sink:
            sink_col = jnp.concatenate([jnp.full((n, 1), sink_ref[h], F32) for h in heads], axis=0)
        o = _softmax_pv(_dot_nt(q_st, k_h), v_h, sink_col).astype(o_ref.dtype)
        for g, h in enumerate(heads):
            o_ref[:, h * HEAD_DIM:(h + 1) * HEAD_DIM] = o[g * n:(g + 1) * n, :]


def _ctx_attn(q, k, v, sink, dims, n_kv, use_sink):
    n_batch, seq, ctx_len = dims
    d = q.shape[1]
    kvw = n_kv * HEAD_DIM
    blk0 = n_batch * seq // ctx_len
    return pl.pallas_call(
        functools.partial(_ctx_attn_kernel, n_kv=n_kv, use_sink=use_sink),
        out_shape=jax.ShapeDtypeStruct((n_batch * ctx_len, d), BF16),
        grid_spec=pltpu.PrefetchScalarGridSpec(
            num_scalar_prefetch=1,
            grid=(n_batch,),
            in_specs=[pl.BlockSpec((ctx_len, d), lambda b, s: (blk0 + b, 0)),
                      pl.BlockSpec((ctx_len, kvw), lambda b, s: (blk0 + b, 0)),
                      pl.BlockSpec((ctx_len, kvw), lambda b, s: (blk0 + b, 0))],
            out_specs=pl.BlockSpec((ctx_len, d), lambda b, s: (b, 0)),
        ),
        compiler_params=_cparams(1),
        name="ctx_attn",
    )(sink, q, k, v)


def _nat_kernel(q_ref, k0_ref, k1_ref, k2_ref, kx_ref, v0_ref, v1_ref, v2_ref, vx_ref, bias_ref, o_ref):
    nq = q_ref.shape[0]
    low = lax.broadcasted_iota(jnp.int32, (nq, LANES), 1) < HEAD_DIM
    for a in range(N_HEADS // 2):
        sl = slice(a * LANES, (a + 1) * LANES)
        qp = q_ref[:, sl]
        k_loc = jnp.concatenate([k0_ref[:, sl], k1_ref[:, sl], k2_ref[:, sl]], axis=0)
        k_ctx = kx_ref[:, sl]
        v_all = jnp.concatenate([v0_ref[:, sl], v1_ref[:, sl], v2_ref[:, sl], vx_ref[:, sl]], axis=0)
        outs = []
        for hh in range(2):
            qm = jnp.where(low if hh == 0 else jnp.logical_not(low), qp, jnp.zeros_like(qp))
            s = jnp.concatenate([_dot_nt(qm, k_loc) + bias_ref[2 * a + hh], _dot_nt(qm, k_ctx)], axis=-1)
            outs.append(_softmax_pv(s, v_all))
        o_ref[:, sl] = jnp.where(low, outs[0], outs[1]).astype(o_ref.dtype)


def _nat_bias_table(rpb, kh):
    g, wr = NAT_ROWS, NAT_WINDOW_ROWS
    qc = jnp.arange(GRID_W)
    kc = jnp.arange(GRID_W)
    q_start = jnp.clip(qc - NA_KW // 2, 0, GRID_W - NA_KW)
    col_ok = (kc[None, :] >= q_start[:, None]) & (kc[None, :] < q_start[:, None] + NA_KW)
    dc = jnp.clip(kc[None, :] - qc[:, None] + NA_KW - 1, 0, 2 * NA_KW - 2)
    j = jnp.arange(g)
    i = jnp.arange(wr)
    dc_hot = (dc[:, :, None] == jnp.arange(2 * NA_KW - 1)).astype(F32)
    by_col = jnp.einsum('hac,qkc->haqk', rpb.astype(F32), dc_hot, precision=HIGHEST)
    tables = []
    for q_rel, rs_rel in ((j, 0 * j), (g + j, j), (wr - g + j, 0 * j + wr - kh)):
        row_ok = (i[None, :] >= rs_rel[:, None]) & (i[None, :] < rs_rel[:, None] + kh)
        dr = jnp.clip(i[None, :] - q_rel[:, None] + NA_KH_MAX - 1, 0, 2 * NA_KH_MAX - 2)
        dr_hot = (dr[:, :, None] == jnp.arange(2 * NA_KH_MAX - 1)).astype(F32)
        b = jnp.einsum('jia,haqk->hjiqk', dr_hot, by_col, precision=HIGHEST)
        b = jnp.where(row_ok[None, :, :, None, None] & col_ok[None, None, None], b, NEG_INF)
        tables.append(b.transpose(0, 1, 3, 2, 4).reshape(N_HEADS, g * GRID_W, wr * GRID_W))
    return jnp.stack(tables)


def _nat_attn(q, k, v, rpb, dims):
    n_batch, seq, ctx_len = dims
    t, d = q.shape
    n_rows = seq // GRID_W
    kh = min(NA_KH_MAX, n_rows)
    g, wr = NAT_ROWS, NAT_WINDOW_ROWS
    n_groups = n_rows // g
    assert kh == NA_KH_MAX and wr == 3 * g and wr >= kh + g - 1 and n_rows % g == 0 and n_groups >= 3
    bias = _nat_bias_table(rpb, kh)
    blk = g * GRID_W
    bpb = seq // blk
    ctx_blk0 = n_batch * seq // ctx_len

    def case_of(m):
        return jnp.where(m == 0, 0, jnp.where(m == n_groups - 1, 2, 1))

    def win(off):
        return lambda m, b: (b * bpb + jnp.clip(m - 1, 0, n_groups - 3) + off, 0)

    tok = pl.BlockSpec((blk, d), lambda m, b: (b * bpb + m, 0))
    ctx = pl.BlockSpec((ctx_len, d), lambda m, b: (ctx_blk0 + b, 0))
    kv = [pl.BlockSpec((blk, d), win(0)), pl.BlockSpec((blk, d), win(1)), pl.BlockSpec((blk, d), win(2)), ctx]
    return pl.pallas_call(
        _nat_kernel,
        out_shape=jax.ShapeDtypeStruct((n_batch * seq, d), BF16),
        grid=(n_groups, n_batch),
        in_specs=[tok] + kv + kv + [
            pl.BlockSpec((None, N_HEADS, blk, wr * GRID_W), lambda m, b: (case_of(m), 0, 0, 0))],
        out_specs=tok,
        compiler_params=_cparams(2),
        name="nat_attn",
    )(q, k, k, k, k, v, v, v, v, bias)


def _bf16_part(a):
    return lax.bitcast_convert_type(lax.bitcast_convert_type(a, jnp.uint32) & jnp.uint32(0xFFFF0000), F32)


def _route_tile(h2, wr_ref, br_ref, rc):
    mt = h2.shape[0]
    h_top = _bf16_part(h2)
    both = _dot_nt(wr_ref[...], h_top.astype(BF16)) + _dot_nt(wr_ref[...], (h2 - h_top).astype(BF16))
    logits = (both[:LANES] + both[LANES:] + br_ref[...])[:ROUTE_ROWS]
    row_i = lax.broadcasted_iota(jnp.int32, logits.shape, 0)
    row = row_i.astype(F32)
    big = jnp.float32(1e9)
    low = jnp.float32(-3e38)
    is_group = row_i < N_GROUPS
    gl = jnp.where(is_group, logits, low)
    gmax = jnp.max(gl, axis=0, keepdims=True)
    g_idx = jnp.min(jnp.where(gl == gmax, row, big), axis=0, keepdims=True)
    gsum = jnp.sum(jnp.where(is_group, jnp.exp(gl - gmax), 0.0), axis=0, keepdims=True)
    g_prob = 1.0 / gsum
    e_lo = N_GROUPS + g_idx * EXPERTS_PER_GROUP
    in_group = (row >= e_lo) & (row < e_lo + EXPERTS_PER_GROUP)
    el = jnp.where(in_group, logits, low)
    v1 = jnp.max(el, axis=0, keepdims=True)
    i1 = jnp.min(jnp.where(el == v1, row, big), axis=0, keepdims=True)
    el2 = jnp.where(row == i1, low, el)
    v2 = jnp.max(el2, axis=0, keepdims=True)
    i2 = jnp.min(jnp.where(el2 == v2, row, big), axis=0, keepdims=True)
    t = jnp.exp(v2 - v1)
    g0 = 1.0 / (1.0 + t) * g_prob
    g1 = t / (1.0 + t) * g_prob

    hit0 = row == i1
    hit1 = row == i2
    onehot = jnp.where(hit0, 1.0, jnp.where(hit1, 1.0, 0.0))
    r_i = lax.broadcasted_iota(jnp.int32, (mt, mt), 0)
    c_i = lax.broadcasted_iota(jnp.int32, (mt, mt), 1)
    earlier = jnp.where(r_i < c_i, 1.0, 0.0).astype(BF16)
    before = jnp.dot(onehot.astype(BF16), earlier, preferred_element_type=F32)
    cnt = jnp.sum(onehot, axis=1, keepdims=True)
    padded = jnp.floor((cnt + (CHUNK - 1)) * (1.0 / CHUNK)) * CHUNK
    a_i = lax.broadcasted_iota(jnp.int32, (ROUTE_ROWS, ROUTE_ROWS), 0)
    b_i = lax.broadcasted_iota(jnp.int32, (ROUTE_ROWS, ROUTE_ROWS), 1)
    lower = jnp.where(b_i < a_i, 1.0, 0.0)
    goff = jnp.dot(lower, jnp.broadcast_to(padded, (ROUTE_ROWS, LANES)), precision=HIGHEST,
                   preferred_element_type=F32)[:, 0:1]
    pos = before + goff
    slot0 = jnp.sum(jnp.where(hit0, pos, 0.0), axis=0, keepdims=True)
    slot1 = jnp.sum(jnp.where(hit1, pos, 0.0), axis=0, keepdims=True)
    return (i1 - N_GROUPS, i2 - N_GROUPS, g0, g1, slot0, slot1), cnt, goff


def _to_slabs(a):
    rows, d = a.shape
    g = a.reshape(rows // CHUNK, CHUNK, d)
    return jnp.concatenate([g[:, :, :d // 2], g[:, :, d // 2:]], axis=1).astype(BF16)


def _from_slabs(s):
    n, _, half = s.shape
    g = s.astype(F32)
    return (g[:, :CHUNK, :].reshape(n * CHUNK, half).astype(BF16), g[:, CHUNK:, :].reshape(n * CHUNK, half).astype(BF16))


def _post_mixer(o_bf, x_ref, mod_ref, g_ref, b_ref, wo_ref, wr_ref, br_ref, x1_ref, xs_ref, route_ref, meta_ref, alpha):
    y = jnp.dot(o_bf, wo_ref[...], preferred_element_type=F32)
    x1 = _layer_norm(alpha * x_ref[...] + mod_ref[2:3, :] * y, g_ref, b_ref)
    x1_ref[...] = x1
    h2 = x1 * (1.0 + mod_ref[4:5, :]) + mod_ref[3:4, :]
    rc = xs_ref.shape[0] * CHUNK
    mt = h2.shape[0]
    rows, cnt, goff = _route_tile(h2, wr_ref, br_ref, rc)

    s_i = lax.broadcasted_iota(jnp.int32, (rc, mt), 0).astype(F32)
    pick = jnp.where(s_i == rows[4], 1.0, jnp.where(s_i == rows[5], 1.0, 0.0)).astype(BF16)
    xs_ref[...] = _to_slabs(jnp.dot(pick, h2.astype(BF16), preferred_element_type=F32))

    sub = lax.broadcasted_iota(jnp.int32, (LANES, mt), 0)
    stacked = jnp.zeros((LANES, mt), F32)
    for j, r in enumerate(rows):
        stacked = jnp.where(sub == j, r, stacked)
    route_ref[...] = stacked.T[:, :ROUTE_W]
    lane = lax.broadcasted_iota(jnp.int32, (ROUTE_ROWS, LANES), 1)
    meta_ref[...] = jnp.where(lane == 0, cnt, jnp.where(lane == 1, goff, 0.0))


def _post_attn_kernel(ol_ref, oc_ref, x_ref, mod_ref, g_ref, b_ref, wo_ref, wr_ref, br_ref, x1_ref, xs_ref, route_ref, meta_ref,
                      *, alpha, n_lat_tiles, n_tiles):
    is_lat = jnp.minimum(pl.program_id(0), n_tiles - 1) < n_lat_tiles
    o = jnp.where(is_lat, ol_ref[...], oc_ref[...])
    _post_mixer(o, x_ref, mod_ref, g_ref, b_ref, wo_ref, wr_ref, br_ref, x1_ref, xs_ref, route_ref, meta_ref, alpha)


def _post_conv_kernel(gb_ref, z_ref, zp_ref, zn_ref, cw_ref, x_ref, mod_ref, g_ref, b_ref, wo_ref, wr_ref, br_ref,
                      x1_ref, xs_ref, route_ref, meta_ref, *, alpha, t_lat, seq, ctx_len, n_tiles):
    i = jnp.minimum(pl.program_id(0), n_tiles - 1)
    tm, d = z_ref.shape
    halo = zp_ref.shape[0]
    z = z_ref[...].astype(F32)
    row = lax.broadcasted_iota(jnp.int32, (tm, 1), 0)
    g = i * tm + row
    pos = jnp.where(g < t_lat, g % seq, g % ctx_len)
    n_seq = jnp.where(g < t_lat, seq, ctx_len)
    prev_row = zp_ref[halo - 1:halo, :].astype(F32)
    next_row = zn_ref[0:1, :].astype(F32)
    z_prev = jnp.where(row == 0, prev_row, pltpu.roll(z, 1, 0))
    z_next = jnp.where(row == tm - 1, next_row, pltpu.roll(z, tm - 1, 0))
    z_prev = jnp.where(pos == 0, 0.0, z_prev)
    z_next = jnp.where(pos == n_seq - 1, 0.0, z_next)
    y = cw_ref[0:1, :] * z_prev + cw_ref[1:2, :] * z + cw_ref[2:3, :] * z_next
    o = (gb_ref[...].astype(F32) * y).astype(BF16)
    _post_mixer(o, x_ref, mod_ref, g_ref, b_ref, wo_ref, wr_ref, br_ref, x1_ref, xs_ref, route_ref, meta_ref, alpha)


def _compact_rows(mt):
    return TOP_K * mt + N_EXPERTS * CHUNK


def _spare_tiles(rc, bm):
    return -(-2 * bm // rc)


def _post_mixer_call(kind, mixer_out, xs, mods, ln_g, ln_b, wo_bf, wr, br, dims, n_tok, alpha, conv_w=None):
    n_batch, seq, ctx_len = dims
    d = xs.shape[1]
    tm = MOE_TILE
    rc = _compact_rows(tm)
    n_tiles = n_tok // tm
    n_lat_tiles = n_batch * seq // tm
    tpb = seq // tm
    seg = functools.partial(_seg_of_tile, n_lat_tiles=n_lat_tiles, tiles_per_batch=tpb, n_batch=n_batch)
    n_spare = _spare_tiles(rc, EXPERT_BLOCK)
    tile = lambda i: jnp.minimum(i, n_tiles - 1)
    tok = pl.BlockSpec((tm, d), lambda i: (tile(i), 0))
    full = lambda a: pl.BlockSpec(a.shape, lambda i: (0,) * a.ndim)
    common_ins = (xs, mods, ln_g, ln_b, wo_bf, wr, br)
    common_specs = [tok, pl.BlockSpec((None, 6, d), lambda i: (seg(tile(i)), 0, 0)), full(ln_g), full(ln_b),
                    full(wo_bf), full(wr), full(br)]
    if kind == 1:
        gb, z = mixer_out
        halo = 16
        hb = tm // halo
        last_blk = z.shape[0] // halo - 1
        ins = (gb, z, z, z, conv_w) + common_ins
        in_specs = [tok, tok,
                    pl.BlockSpec((halo, d), lambda i: (jnp.maximum(tile(i) * hb - 1, 0), 0)),
                    pl.BlockSpec((halo, d), lambda i: (jnp.minimum((tile(i) + 1) * hb, last_blk), 0)),
                    full(conv_w)] + common_specs
        body = functools.partial(_post_conv_kernel, alpha=alpha, t_lat=n_batch * seq, seq=seq, ctx_len=ctx_len,
                                 n_tiles=n_tiles)
    else:
        o_lat, o_ctx = mixer_out
        n_ctx_tiles = o_ctx.shape[0] // tm
        ins = (o_lat, o_ctx) + common_ins
        in_specs = [pl.BlockSpec((tm, d), lambda i: (jnp.minimum(tile(i), n_lat_tiles - 1), 0)),
                    pl.BlockSpec((tm, d), lambda i: (jnp.clip(tile(i) - n_lat_tiles, 0, n_ctx_tiles - 1), 0))] + common_specs
        body = functools.partial(_post_attn_kernel, alpha=alpha, n_lat_tiles=n_lat_tiles, n_tiles=n_tiles)
    return pl.pallas_call(
        body,
        out_shape=[jax.ShapeDtypeStruct((n_tok, d), F32),
                   jax.ShapeDtypeStruct(((n_tiles + n_spare) * rc // CHUNK, 2 * CHUNK, d // 2), BF16),
                   jax.ShapeDtypeStruct((n_tok, ROUTE_W), F32),
                   jax.ShapeDtypeStruct((n_tiles, ROUTE_ROWS, LANES), F32)],
        grid=(n_tiles + n_spare,),
        in_specs=in_specs,
        out_specs=[tok, pl.BlockSpec((rc // CHUNK, 2 * CHUNK, d // 2), lambda i: (i, 0, 0)),
                   pl.BlockSpec((tm, ROUTE_W), lambda i: (tile(i), 0)),
                   pl.BlockSpec((None, ROUTE_ROWS, LANES), lambda i: (tile(i), 0, 0))],
        compiler_params=_cparams(1),
        name=f"post_mixer_{kind}",
    )(*ins)


def _expert_schedule(meta, rc, bm):
    n_tiles = meta.shape[0]
    cpb = bm // CHUNK
    cnt = meta[:, N_GROUPS:N_GROUPS + N_EXPERTS, 0].astype(jnp.int32)
    goff = meta[:, N_GROUPS:N_GROUPS + N_EXPERTS, 1].astype(jnp.int32)
    nch = ((cnt + CHUNK - 1) // CHUNK).T
    incl = jnp.cumsum(nch, axis=1)
    per_e = incl[:, -1]
    nbk = (per_e + cpb - 1) // cpb
    blk_start = jnp.cumsum(nbk) - nbk
    src0 = (jnp.arange(n_tiles, dtype=jnp.int32)[None, :] * rc + goff.T) // CHUNK
    n_blocks = (TOP_K * n_tiles * MOE_TILE + (CHUNK - 1) * N_EXPERTS * n_tiles) // CHUNK // cpb + N_EXPERTS
    blk = jnp.arange(n_blocks, dtype=jnp.int32)
    blk_e = jnp.clip(jnp.sum(blk_start[None, :] <= blk[:, None], axis=1) - 1, 0, N_EXPERTS - 1).astype(jnp.int32)
    onehot = (blk_e[:, None] == jnp.arange(N_EXPERTS)[None, :]).astype(F32)
    tables = jnp.concatenate([incl, incl - nch, src0, blk_start[:, None], per_e[:, None]], axis=1).astype(F32)
    per_blk = jnp.round(jnp.dot(onehot, tables, precision=HIGHEST)).astype(jnp.int32)
    incl_b, excl_b, src_b = (per_blk[:, None, i * n_tiles:(i + 1) * n_tiles] for i in range(3))
    start_b, total_b = per_blk[:, 3 * n_tiles], per_blk[:, 3 * n_tiles + 1]
    lq = ((blk - start_b) * cpb)[:, None] + jnp.arange(cpb, dtype=jnp.int32)[None, :]
    run = jnp.sum(incl_b <= lq[:, :, None], axis=2)
    pick = jnp.arange(n_tiles, dtype=jnp.int32)[None, None, :] == run[:, :, None]
    first = jnp.sum(jnp.where(pick, excl_b, 0), axis=2)
    src = jnp.sum(jnp.where(pick, src_b, 0), axis=2)
    chunk_src = jnp.where(lq < total_b[:, None], src + (lq - first), -1).astype(jnp.int32).reshape(-1)
    n_used = jnp.sum(nbk).astype(jnp.int32).reshape(1)
    return chunk_src, blk_e, n_used, n_blocks


def _expert_kernel(blk_e_ref, src_ref, n_used_ref, x_hbm, wg_ref, wu_ref, wd_ref, y_hbm, xbuf, ybuf, gsem, ssem,
                   wgu_bf, wd_bf, *, zero_slab, trash_slab):
    b = pl.program_id(0)
    nb = pl.num_programs(0)
    _, cpb, _, half = xbuf.shape
    ff = wg_ref.shape[1]
    n_used = n_used_ref[0]

    def gather(blk, slot, real):
        base = jnp.where(real, blk, 0) * cpb
        for j in range(cpb):
            v = jnp.where(real, src_ref[base + j], -1)
            pltpu.make_async_copy(x_hbm.at[pl.ds(jnp.where(v < 0, zero_slab, v), 1)], xbuf.at[slot, pl.ds(j, 1)],
                                  gsem.at[slot]).start()

    def scatter(blk, slot, real):
        base = jnp.where(real, blk, 0) * cpb
        for j in range(cpb):
            v = jnp.where(real, src_ref[base + j], -1)
            dst = jnp.where(v < 0, trash_slab + slot * cpb + j, v)
            pltpu.make_async_copy(ybuf.at[slot, pl.ds(j, 1)], y_hbm.at[pl.ds(dst, 1)], ssem.at[slot]).start()

    def wait_gather(slot):
        pltpu.make_async_copy(x_hbm.at[pl.ds(0, cpb)], xbuf.at[slot], gsem.at[slot]).wait()

    def wait_scatter(slot):
        pltpu.make_async_copy(ybuf.at[slot], y_hbm.at[pl.ds(0, cpb)], ssem.at[slot]).wait()

    in_use = b < n_used

    @pl.when((b == 0) & in_use)
    def _():
        ybuf[...] = jnp.zeros_like(ybuf)
        scatter(0, 0, False)
        scatter(0, 1, False)
        gather(0, 0, True)

    @pl.when(in_use & ((b == 0) | (blk_e_ref[b] != blk_e_ref[jnp.maximum(b - 1, 0)])))
    def _():
        wgu_bf[:, :ff] = wg_ref[...].astype(BF16)
        wgu_bf[:, ff:] = wu_ref[...].astype(BF16)
        wd_bf[...] = wd_ref[...].astype(BF16)

    @pl.when(in_use)
    def _():
        slot = b % 2
        wait_gather(slot)
        wait_scatter(slot)
        gather(b + 1, 1 - slot, b + 1 < n_used)
        for part in range(EXPERT_PARTS):
            slabs = pl.ds(part * (cpb // EXPERT_PARTS), cpb // EXPERT_PARTS)
            xa, xb = _from_slabs(xbuf[slot, slabs])
            gu = (jnp.dot(xa, wgu_bf[:half, :], preferred_element_type=F32)
                  + jnp.dot(xb, wgu_bf[half:, :], preferred_element_type=F32))
            gate = gu[:, :ff]
            act = gate / (1.0 + jnp.exp(-gate)) * gu[:, ff:]
            ybuf[slot, slabs] = _to_slabs(jnp.dot(act.astype(BF16), wd_bf[...], preferred_element_type=F32))
        scatter(b, slot, True)

    @pl.when((b == nb - 1) & (n_used > 0))
    def _():
        wait_gather(n_used % 2)
        wait_scatter(0)
        wait_scatter(1)


def _experts(xs_c, chunk_src, blk_e, n_used, n_blocks, w_gate, w_up, w_down, layer, rc, bm):
    d = 2 * xs_c.shape[2]
    ff = w_gate.shape[-1]
    n_tiles = xs_c.shape[0] * CHUNK // rc - _spare_tiles(rc, bm)
    assert rc - TRASH_ROWS >= TOP_K * MOE_TILE + N_EXPERTS * (CHUNK - 1) and bm % (EXPERT_PARTS * CHUNK) == 0
    return pl.pallas_call(
        functools.partial(_expert_kernel, zero_slab=(rc - TRASH_ROWS) // CHUNK, trash_slab=n_tiles * rc // CHUNK),
        out_shape=jax.ShapeDtypeStruct(xs_c.shape, xs_c.dtype),
        grid_spec=pltpu.PrefetchScalarGridSpec(
            num_scalar_prefetch=3,
            grid=(n_blocks,),
            in_specs=[
                pl.BlockSpec(memory_space=pl.ANY),
                pl.BlockSpec((None, None, d, ff), lambda b, be, cs, nu: (layer, be[b], 0, 0)),
                pl.BlockSpec((None, None, d, ff), lambda b, be, cs, nu: (layer, be[b], 0, 0)),
                pl.BlockSpec((None, None, ff, d), lambda b, be, cs, nu: (layer, be[b], 0, 0)),
            ],
            out_specs=pl.BlockSpec(memory_space=pl.ANY),
            scratch_shapes=[
                pltpu.VMEM((2, bm // CHUNK, 2 * CHUNK, d // 2), BF16),
                pltpu.VMEM((2, bm // CHUNK, 2 * CHUNK, d // 2), BF16),
                pltpu.SemaphoreType.DMA((2,)),
                pltpu.SemaphoreType.DMA((2,)),
                pltpu.VMEM((d, 2 * ff), BF16),
                pltpu.VMEM((ff, d), BF16),
            ],
        ),
        input_output_aliases={3: 0},
        compiler_params=_cparams(1),
        name="moe_experts",
    )(blk_e, chunk_src, n_used, xs_c, w_gate, w_up, w_down)


def _combine_kernel(ys_ref, x1_ref, route_ref, mod_ref, g_ref, b_ref, o_ref, *, alpha):
    rc = ys_ref.shape[0] * CHUNK
    mt = x1_ref.shape[0]
    ya, yb = _from_slabs(ys_ref[...])
    route = route_ref[...]
    s_i = lax.broadcasted_iota(jnp.int32, (mt, rc), 1).astype(F32)
    y = None
    for k in range(TOP_K):
        pick = jnp.where(s_i == route[:, 4 + k:5 + k], 1.0, 0.0).astype(BF16)
        yk = route[:, 2 + k:3 + k] * jnp.concatenate([jnp.dot(pick, ya, preferred_element_type=F32),
                                                      jnp.dot(pick, yb, preferred_element_type=F32)], axis=1)
        y = yk if y is None else y + yk
    o_ref[...] = _layer_norm(alpha * x1_ref[...] + mod_ref[5:6, :] * y, g_ref, b_ref)


def _combine(ys_c, x1, route, mods, ln_g, ln_b, dims, n_tok, alpha, rc):
    n_batch, seq, ctx_len = dims
    d = x1.shape[1]
    tg = MOE_TILE
    n_lat_tiles = n_batch * seq // tg
    tpb = seq // tg
    seg = functools.partial(_seg_of_tile, n_lat_tiles=n_lat_tiles, tiles_per_batch=tpb, n_batch=n_batch)
    return pl.pallas_call(
        functools.partial(_combine_kernel, alpha=alpha),
        out_shape=jax.ShapeDtypeStruct((n_tok, d), F32),
        grid=(n_tok // tg,),
        in_specs=[
            pl.BlockSpec((rc // CHUNK, 2 * CHUNK, d // 2), lambda i: (i, 0, 0)),
            pl.BlockSpec((tg, d), lambda i: (i, 0)),
            pl.BlockSpec((tg, ROUTE_W), lambda i: (i, 0)),
            pl.BlockSpec((None, 6, d), lambda i: (seg(i), 0, 0)),
            pl.BlockSpec(ln_g.shape, lambda i: (0, 0)),
            pl.BlockSpec(ln_b.shape, lambda i: (0, 0)),
        ],
        out_specs=pl.BlockSpec((tg, d), lambda i: (i, 0)),
        compiler_params=_cparams(1),
        name="moe_combine",
    )(ys_c, x1, route, mods, ln_g, ln_b)


def kernel(x, c, ctx, c_ctx, w_ada, b_ada, ln1_g, ln1_b, ln2_g, ln2_b, attn_w_qkv, attn_w_o, attn_sink, conv_w_in, conv_w,
           conv_w_out, nat_w_qkv, nat_w_o, nat_rpb, router_w_group, router_b_group, router_w_expert, router_b_expert,
           expert_w_gate, expert_w_up, expert_w_down):
    n_batch, seq, d = x.shape
    ctx_len = ctx.shape[1]
    depth = w_ada.shape[0]
    dims = (n_batch, seq, ctx_len)
    t_lat = n_batch * seq
    t_all = t_lat + n_batch * ctx_len
    alpha = float((2 * depth) ** 0.25)
    assert d == D_MODEL and n_batch + 1 <= ADA_ROWS
    assert seq % TOKEN_TILE == 0 and (n_batch * ctx_len) % TOKEN_TILE == 0 and seq % GRID_W == 0
    assert TOKEN_TILE % ctx_len == 0 or ctx_len % TOKEN_TILE == 0
    rc = _compact_rows(MOE_TILE)

    cc = jnp.zeros((ADA_ROWS, d), F32).at[:n_batch].set(c).at[n_batch].set(c_ctx)
    mods_all = _ada(cc, w_ada, b_ada).reshape(depth, ADA_ROWS, 6, d)
    rope_tabs = _rope_tables(seq, TOKEN_TILE)
    xs = jnp.concatenate([x.reshape(t_lat, d), ctx.reshape(n_batch * ctx_len, d)], axis=0)

    n_route_pad = LANES - N_GROUPS - N_EXPERTS
    for i in range(depth):
        last = i == depth - 1
        j = i // 3
        kind = i % 3
        mods = mods_all[i]
        n_tok = t_lat if last else t_all
        g1, b1 = ln1_g[i].reshape(1, d), ln1_b[i].reshape(1, d)
        g2, b2 = ln2_g[i].reshape(1, d), ln2_b[i].reshape(1, d)
        wr = jnp.concatenate([router_w_group[i], router_w_expert[i], jnp.zeros((d, n_route_pad), F32)], axis=1).T
        wr_top = _bf16_part(wr)
        wr = jnp.concatenate([wr_top.astype(BF16), (wr - wr_top).astype(BF16)], axis=0)
        br = jnp.concatenate([router_b_group[i], router_b_expert[i], jnp.zeros((n_route_pad,), F32)]).reshape(LANES, 1)

        if kind == 0:
            q, k, v = _proj_in(0, xs, mods, attn_w_qkv[j].astype(BF16), dims, rope_tabs)
            o = _window_attn(q, k, v, attn_sink[j], dims)
            o_ctx = o if last else _ctx_attn(q, k, v, attn_sink[j], dims, A_KV_HEADS, True)
            post = _post_mixer_call(0, (o, o_ctx), xs, mods, g1, b1, attn_w_o[j].astype(BF16), wr, br, dims, n_tok, alpha)
        elif kind == 1:
            gb, z = _proj_in(1, xs, mods, conv_w_in[j].astype(BF16), dims)
            post = _post_mixer_call(1, (gb, z), xs, mods, g1, b1, conv_w_out[j].astype(BF16), wr, br, dims, n_tok, alpha,
                                    conv_w=conv_w[j])
        else:
            q, k, v = _proj_in(2, xs, mods, nat_w_qkv[j].astype(BF16), dims)
            o = _nat_attn(q, k, v, nat_rpb[j], dims)
            o_ctx = o if last else _ctx_attn(q, k, v, jnp.zeros((N_HEADS,), F32), dims, N_HEADS, False)
            post = _post_mixer_call(2, (o, o_ctx), xs, mods, g1, b1, nat_w_o[j].astype(BF16), wr, br, dims, n_tok, alpha)

        x1, xs_c, route, meta = post
        chunk_src, blk_e, n_used, n_blocks = _expert_schedule(meta, rc, EXPERT_BLOCK)
        ys_c = _experts(xs_c, chunk_src, blk_e, n_used, n_blocks, expert_w_gate, expert_w_up, expert_w_down, i, rc,
                        EXPERT_BLOCK)
        xs = _combine(ys_c, x1, route, mods, g2, b2, dims, n_tok, alpha, rc)
    return xs[:t_lat].reshape(n_batch, seq, d)
```

```python
import functools

import jax
import jax.numpy as jnp
from jax import lax
from jax.experimental import pallas as pl
from jax.experimental.pallas import tpu as pltpu

F32 = jnp.float32
BF16 = jnp.bfloat16
HIGHEST = lax.Precision.HIGHEST

D_MODEL = 1024
HEAD_DIM = 64
N_HEADS = D_MODEL // HEAD_DIM
A_KV_HEADS = N_HEADS // 4
A_KV = A_KV_HEADS * HEAD_DIM
A_WINDOW = 128
A_BLOCK = 128
GRID_W = 64
ROPE_BASE = 10000.0
NA_KH_MAX = 8
NA_KW = 16
N_GROUPS = 4
EXPERTS_PER_GROUP = 8
N_EXPERTS = N_GROUPS * EXPERTS_PER_GROUP
TOP_K = 2
EXPERT_FF = D_MODEL // 2
LN_EPS = 1e-5
NEG_INF = -1e30

LANES = 128
ADA_ROWS = 24
ROUTE_W = 8
TOKEN_TILE = 512
MOE_TILE = 256
EXPERT_BLOCK = 512
EXPERT_PARTS = 2
CHUNK = 8
TRASH_ROWS = 32
ROUTE_ROWS = 64
NAT_ROWS = 4
NAT_WINDOW_ROWS = 12
VMEM_LIMIT = 56 * 1024 * 1024


def _cparams(n_axes=1):
    return pltpu.CompilerParams(dimension_semantics=("arbitrary",) * n_axes, vmem_limit_bytes=VMEM_LIMIT)


def _ada_kernel(c_ref, w_ref, b_ref, o_ref):
    cc = c_ref[...]
    s = cc / (1.0 + jnp.exp(-cc))
    o_ref[...] = jnp.dot(s, w_ref[...], precision=HIGHEST, preferred_element_type=F32) + b_ref[...]


def _ada(cc, w_ada, b_ada):
    depth, d, n = w_ada.shape
    nt = n // 4
    return pl.pallas_call(
        _ada_kernel,
        out_shape=jax.ShapeDtypeStruct((depth, ADA_ROWS, n), F32),
        grid=(depth, n // nt),
        in_specs=[
            pl.BlockSpec((ADA_ROWS, d), lambda i, j: (0, 0)),
            pl.BlockSpec((None, d, nt), lambda i, j: (i, 0, j)),
            pl.BlockSpec((None, 1, nt), lambda i, j: (i, 0, j)),
        ],
        out_specs=pl.BlockSpec((None, ADA_ROWS, nt), lambda i, j: (i, 0, j)),
        compiler_params=_cparams(2),
        name="ada_mod",
    )(cc, w_ada, b_ada.reshape(depth, 1, n))


def _modulated(x_ref, mod_ref, shift_row):
    return x_ref[...] * (1.0 + mod_ref[shift_row + 1:shift_row + 2, :]) + mod_ref[shift_row:shift_row + 1, :]


def _layer_norm(r, g_ref, b_ref):
    mu = jnp.mean(r, axis=-1, keepdims=True)
    rc = r - mu
    var = jnp.mean(rc * rc, axis=-1, keepdims=True)
    return rc * lax.rsqrt(var + LN_EPS) * g_ref[...] + b_ref[...]


def _dot_nt(a, b):
    return lax.dot_general(a, b, (((1,), (1,)), ((), ())), preferred_element_type=F32)


def _softmax_pv(s, v, sink_col=None):
    m = jnp.max(s, axis=-1, keepdims=True)
    if sink_col is not None:
        m = jnp.maximum(m, sink_col)
    e = jnp.exp(s - m)
    den = jnp.sum(e, axis=-1, keepdims=True)
    if sink_col is not None:
        den = den + jnp.exp(sink_col - m)
    return jnp.dot(e.astype(BF16), v, preferred_element_type=F32) / den


def _softmax_pv_ones(s, v_ones, nv):
    m = jnp.max(s, axis=-1, keepdims=True)
    both = jnp.dot(jnp.exp(s - m).astype(BF16), v_ones, preferred_element_type=F32)
    return both[:, :nv] / both[:, nv:nv + 1]


def _seg_of_tile(i, n_lat_tiles, tiles_per_batch, n_batch):
    return jnp.where(i < n_lat_tiles, i // tiles_per_batch, n_batch)


def _proj_attn_kernel(x_ref, mod_ref, cos_ref, sin_ref, w_ref, q_ref, k_ref, v_ref):
    h = _modulated(x_ref, mod_ref, 0).astype(BF16)
    cos = cos_ref[...]
    sin = sin_ref[...]
    lane = lax.broadcasted_iota(jnp.int32, cos.shape, 1)
    low_half = (lane % 32) < 16

    def rope(a):
        partner = jnp.where(low_half, pltpu.roll(a, LANES - 16, 1), pltpu.roll(a, 16, 1))
        return a * cos + partner * sin

    def rope_cols(c0, width):
        a = jnp.dot(h, w_ref[:, c0:c0 + width], preferred_element_type=F32)
        return jnp.concatenate([rope(a[:, j * LANES:(j + 1) * LANES]) for j in range(width // LANES)], axis=1)

    nq = q_ref.shape[1]
    nk = k_ref.shape[1]
    wide = 2 * LANES
    for j in range(nq // wide):
        q_ref[:, j * wide:(j + 1) * wide] = (rope_cols(j * wide, wide) * (HEAD_DIM ** -0.5)).astype(BF16)
    for j in range(nk // wide):
        k_ref[:, j * wide:(j + 1) * wide] = rope_cols(nq + j * wide, wide).astype(BF16)
    v_ref[...] = jnp.dot(h, w_ref[:, nq + nk:], preferred_element_type=F32).astype(BF16)


def _proj_nat_kernel(x_ref, mod_ref, w_ref, q_ref, k_ref, v_ref):
    h = _modulated(x_ref, mod_ref, 0).astype(BF16)
    d = q_ref.shape[1]
    q_ref[...] = (jnp.dot(h, w_ref[:, :d], preferred_element_type=F32) * (HEAD_DIM ** -0.5)).astype(BF16)
    k_ref[...] = jnp.dot(h, w_ref[:, d:2 * d], preferred_element_type=F32).astype(BF16)
    v_ref[...] = jnp.dot(h, w_ref[:, 2 * d:], preferred_element_type=F32).astype(BF16)


def _proj_conv_kernel(x_ref, mod_ref, w_ref, gb_ref, z_ref):
    h = _modulated(x_ref, mod_ref, 0).astype(BF16)
    d = gb_ref.shape[1]
    gb_ref[...] = jnp.dot(h, w_ref[:, :d], preferred_element_type=F32).astype(BF16)
    gc = jnp.dot(h, w_ref[:, d:2 * d], preferred_element_type=F32)
    u = jnp.dot(h, w_ref[:, 2 * d:], preferred_element_type=F32)
    z_ref[...] = (gc * u).astype(BF16)


def _proj_in(kind, xs, mods, w_bf, dims, rope_tabs=None):
    n_batch, seq, ctx_len = dims
    t, d = xs.shape
    tm = TOKEN_TILE
    n_lat_tiles = n_batch * seq // tm
    tpb = seq // tm
    seg = functools.partial(_seg_of_tile, n_lat_tiles=n_lat_tiles, tiles_per_batch=tpb, n_batch=n_batch)
    x_spec = pl.BlockSpec((tm, d), lambda i: (i, 0))
    mod_spec = pl.BlockSpec((None, 6, d), lambda i: (seg(i), 0, 0))
    w_spec = pl.BlockSpec(w_bf.shape, lambda i: (0, 0))

    def tok_out(n):
        return jax.ShapeDtypeStruct((t, n), BF16), pl.BlockSpec((tm, n), lambda i: (i, 0))

    if kind == 0:
        cos, sin = rope_tabs
        tab_spec = pl.BlockSpec((tm, LANES), lambda i: (jnp.where(i < n_lat_tiles, i % tpb, tpb), 0))
        outs = [tok_out(d), tok_out(A_KV), tok_out(A_KV)]
        body, ins, in_specs = _proj_attn_kernel, (xs, mods, cos, sin, w_bf), [x_spec, mod_spec, tab_spec, tab_spec, w_spec]
    elif kind == 1:
        outs = [tok_out(d), tok_out(d)]
        body, ins, in_specs = _proj_conv_kernel, (xs, mods, w_bf), [x_spec, mod_spec, w_spec]
    else:
        outs = [tok_out(d), tok_out(d), tok_out(d)]
        body, ins, in_specs = _proj_nat_kernel, (xs, mods, w_bf), [x_spec, mod_spec, w_spec]
    return pl.pallas_call(
        body,
        out_shape=[o[0] for o in outs],
        grid=(t // tm,),
        in_specs=in_specs,
        out_specs=[o[1] for o in outs],
        compiler_params=_cparams(1),
        name=f"proj_in_{kind}",
    )(*ins)


def _rope_tables(seq, tm):
    quarter = HEAD_DIM // 4
    inv_freq = ROPE_BASE ** (-jnp.arange(quarter, dtype=F32) / quarter)
    tpos = jnp.arange(seq)
    ang_r = (tpos // GRID_W).astype(F32)[:, None] * inv_freq
    ang_c = (tpos % GRID_W).astype(F32)[:, None] * inv_freq
    cos_h = jnp.concatenate([jnp.cos(ang_r), jnp.cos(ang_r), jnp.cos(ang_c), jnp.cos(ang_c)], axis=1)
    sin_h = jnp.concatenate([-jnp.sin(ang_r), jnp.sin(ang_r), -jnp.sin(ang_c), jnp.sin(ang_c)], axis=1)
    reps = LANES // HEAD_DIM
    cos = jnp.concatenate([jnp.tile(cos_h, (1, reps)), jnp.ones((tm, LANES), F32)], axis=0)
    sin = jnp.concatenate([jnp.tile(sin_h, (1, reps)), jnp.zeros((tm, LANES), F32)], axis=0)
    return cos, sin


def _window_attn_kernel(sink_ref, q_ref, kp_ref, kc_ref, kn_ref, kx_ref, vp_ref, vc_ref, vn_ref, vx_ref, o_ref, *, seq):
    qi = pl.program_id(1)
    blk = A_BLOCK
    group = N_HEADS // A_KV_HEADS
    kcat = jnp.concatenate([kp_ref[...], kc_ref[...], kn_ref[...], kx_ref[...]], axis=0)
    vcat = jnp.concatenate([vp_ref[...], vc_ref[...], vn_ref[...], vx_ref[...]], axis=0)
    rows = group * blk
    row = lax.broadcasted_iota(jnp.int32, (rows, blk), 0) & (blk - 1)
    col = lax.broadcasted_iota(jnp.int32, (rows, blk), 1)
    prev_ok = (col + (blk - A_WINDOW) >= row) & (qi > 0)
    next_ok = (col - (blk - A_WINDOW) <= row) & (qi < seq // blk - 1)
    for kk in range(A_KV_HEADS):
        k_h = kcat[:, kk * HEAD_DIM:(kk + 1) * HEAD_DIM]
        v_h = vcat[:, kk * HEAD_DIM:(kk + 1) * HEAD_DIM]
        heads = [kk * group + g for g in range(group)]
        q_st = jnp.concatenate([q_ref[:, h * HEAD_DIM:(h + 1) * HEAD_DIM] for h in heads], axis=0)
        sink_col = jnp.concatenate([jnp.full((blk, 1), sink_ref[h], F32) for h in heads], axis=0)
        s = _dot_nt(q_st, k_h)
        s = jnp.concatenate([jnp.where(prev_ok, s[:, :blk], NEG_INF), s[:, blk:2 * blk],
                             jnp.where(next_ok, s[:, 2 * blk:3 * blk], NEG_INF), s[:, 3 * blk:]], axis=1)
        o = _softmax_pv(s, v_h, sink_col).astype(o_ref.dtype)
        for g, h in enumerate(heads):
            o_ref[:, h * HEAD_DIM:(h + 1) * HEAD_DIM] = o[g * blk:(g + 1) * blk, :]


def _window_attn(q, k, v, sink, dims):
    n_batch, seq, ctx_len = dims
    t, d = q.shape
    blk = A_BLOCK
    nb = seq // blk
    t_lat = n_batch * seq
    ctx_blk0 = t_lat // ctx_len

    def q_map(b, i, s):
        return (b * nb + i, 0)

    def kv_map(off):
        return lambda b, i, s: (b * nb + jnp.clip(i + off, 0, nb - 1), 0)

    def ctx_map(b, i, s):
        return (ctx_blk0 + b, 0)

    kv_specs = [pl.BlockSpec((blk, A_KV), kv_map(-1)), pl.BlockSpec((blk, A_KV), kv_map(0)),
                pl.BlockSpec((blk, A_KV), kv_map(1)), pl.BlockSpec((ctx_len, A_KV), ctx_map)]
    return pl.pallas_call(
        functools.partial(_window_attn_kernel, seq=seq),
        out_shape=jax.ShapeDtypeStruct((t_lat, d), BF16),
        grid_spec=pltpu.PrefetchScalarGridSpec(
            num_scalar_prefetch=1,
            grid=(n_batch, nb),
            in_specs=[pl.BlockSpec((blk, d), q_map)] + kv_specs + kv_specs,
            out_specs=pl.BlockSpec((blk, d), q_map),
        ),
        compiler_params=_cparams(2),
        name="window_attn",
    )(sink, q, k, k, k, k, v, v, v, v)


def _ctx_attn_kernel(sink_ref, q_ref, k_ref, v_ref, o_ref, *, n_kv, use_sink):
    group = N_HEADS // n_kv
    n = q_ref.shape[0]
    for kk in range(n_kv):
        k_h = k_ref[:, kk * HEAD_DIM:(kk + 1) * HEAD_DIM]
        v_h = v_ref[:, kk * HEAD_DIM:(kk + 1) * HEAD_DIM]
        heads = [kk * group + g for g in range(group)]
        q_st = jnp.concatenate([q_ref[:, h * HEAD_DIM:(h + 1) * HEAD_DIM] for h in heads], axis=0)
        sink_col = None
        if use_sink:
            sink_col = jnp.concatenate([jnp.full((n, 1), sink_ref[h], F32) for h in heads], axis=0)
        o = _softmax_pv(_dot_nt(q_st, k_h), v_h, sink_col).astype(o_ref.dtype)
        for g, h in enumerate(heads):
            o_ref[:, h * HEAD_DIM:(h + 1) * HEAD_DIM] = o[g * n:(g + 1) * n, :]


def _ctx_attn(q, k, v, sink, dims, n_kv, use_sink):
    n_batch, seq, ctx_len = dims
    d = q.shape[1]
    kvw = n_kv * HEAD_DIM
    blk0 = n_batch * seq // ctx_len
    return pl.pallas_call(
        functools.partial(_ctx_attn_kernel, n_kv=n_kv, use_sink=use_sink),
        out_shape=jax.ShapeDtypeStruct((n_batch * ctx_len, d), BF16),
        grid_spec=pltpu.PrefetchScalarGridSpec(
            num_scalar_prefetch=1,
            grid=(n_batch,),
            in_specs=[pl.BlockSpec((ctx_len, d), lambda b, s: (blk0 + b, 0)),
                      pl.BlockSpec((ctx_len, kvw), lambda b, s: (blk0 + b, 0)),
                      pl.BlockSpec((ctx_len, kvw), lambda b, s: (blk0 + b, 0))],
            out_specs=pl.BlockSpec((ctx_len, d), lambda b, s: (b, 0)),
        ),
        compiler_params=_cparams(1),
        name="ctx_attn",
    )(sink, q, k, v)


def _nat_kernel(q_ref, k0_ref, k1_ref, k2_ref, kx_ref, v0_ref, v1_ref, v2_ref, vx_ref, bias_ref, o_ref):
    nq = q_ref.shape[0]
    low = lax.broadcasted_iota(jnp.int32, (nq, LANES), 1) < HEAD_DIM
    for a in range(N_HEADS // 2):
        sl = slice(a * LANES, (a + 1) * LANES)
        qp = q_ref[:, sl]
        k_loc = jnp.concatenate([k0_ref[:, sl], k1_ref[:, sl], k2_ref[:, sl]], axis=0)
        k_ctx = kx_ref[:, sl]
        v_all = jnp.concatenate([v0_ref[:, sl], v1_ref[:, sl], v2_ref[:, sl], vx_ref[:, sl]], axis=0)
        v_all = jnp.concatenate([v_all, jnp.ones_like(v_all)], axis=1)
        outs = []
        for hh in range(2):
            qm = jnp.where(low if hh == 0 else jnp.logical_not(low), qp, jnp.zeros_like(qp))
            s = jnp.concatenate([_dot_nt(qm, k_loc) + bias_ref[2 * a + hh], _dot_nt(qm, k_ctx)], axis=-1)
            outs.append(_softmax_pv_ones(s, v_all, LANES))
        o_ref[:, sl] = jnp.where(low, outs[0], outs[1]).astype(o_ref.dtype)


def _nat_bias_table(rpb, kh):
    g, wr = NAT_ROWS, NAT_WINDOW_ROWS
    qc = jnp.arange(GRID_W)
    kc = jnp.arange(GRID_W)
    q_start = jnp.clip(qc - NA_KW // 2, 0, GRID_W - NA_KW)
    col_ok = (kc[None, :] >= q_start[:, None]) & (kc[None, :] < q_start[:, None] + NA_KW)
    dc = jnp.clip(kc[None, :] - qc[:, None] + NA_KW - 1, 0, 2 * NA_KW - 2)
    j = jnp.arange(g)
    i = jnp.arange(wr)
    dc_hot = (dc[:, :, None] == jnp.arange(2 * NA_KW - 1)).astype(F32)
    by_col = jnp.einsum('hac,qkc->haqk', rpb.astype(F32), dc_hot, precision=HIGHEST)
    tables = []
    for q_rel, rs_rel in ((j, 0 * j), (g + j, j), (wr - g + j, 0 * j + wr - kh)):
        row_ok = (i[None, :] >= rs_rel[:, None]) & (i[None, :] < rs_rel[:, None] + kh)
        dr = jnp.clip(i[None, :] - q_rel[:, None] + NA_KH_MAX - 1, 0, 2 * NA_KH_MAX - 2)
        dr_hot = (dr[:, :, None] == jnp.arange(2 * NA_KH_MAX - 1)).astype(F32)
        b = jnp.einsum('jia,haqk->hjqik', dr_hot, by_col, precision=HIGHEST)
        b = jnp.where(row_ok[None, :, None, :, None] & col_ok[None, None, :, None, :], b, NEG_INF)
        tables.append(b.reshape(N_HEADS, g * GRID_W, wr * GRID_W))
    return jnp.stack(tables)


def _nat_attn(q, k, v, rpb, dims):
    n_batch, seq, ctx_len = dims
    t, d = q.shape
    n_rows = seq // GRID_W
    kh = min(NA_KH_MAX, n_rows)
    g, wr = NAT_ROWS, NAT_WINDOW_ROWS
    n_groups = n_rows // g
    assert kh == NA_KH_MAX and wr == 3 * g and wr >= kh + g - 1 and n_rows % g == 0 and n_groups >= 3
    bias = _nat_bias_table(rpb, kh)
    blk = g * GRID_W
    bpb = seq // blk
    ctx_blk0 = n_batch * seq // ctx_len

    def case_of(m):
        return jnp.where(m == 0, 0, jnp.where(m == n_groups - 1, 2, 1))

    def win(off):
        return lambda m, b: (b * bpb + jnp.clip(m - 1, 0, n_groups - 3) + off, 0)

    tok = pl.BlockSpec((blk, d), lambda m, b: (b * bpb + m, 0))
    ctx = pl.BlockSpec((ctx_len, d), lambda m, b: (ctx_blk0 + b, 0))
    kv = [pl.BlockSpec((blk, d), win(0)), pl.BlockSpec((blk, d), win(1)), pl.BlockSpec((blk, d), win(2)), ctx]
    return pl.pallas_call(
        _nat_kernel,
        out_shape=jax.ShapeDtypeStruct((n_batch * seq, d), BF16),
        grid=(n_groups, n_batch),
        in_specs=[tok] + kv + kv + [
            pl.BlockSpec((None, N_HEADS, blk, wr * GRID_W), lambda m, b: (case_of(m), 0, 0, 0))],
        out_specs=tok,
        compiler_params=_cparams(2),
        name="nat_attn",
    )(q, k, k, k, k, v, v, v, v, bias)


def _bf16_part(a):
    return lax.bitcast_convert_type(lax.bitcast_convert_type(a, jnp.uint32) & jnp.uint32(0xFFFF0000), F32)


def _route_tile(h2, wr_ref, br_ref, rc):
    mt = h2.shape[0]
    h_top = _bf16_part(h2)
    both = _dot_nt(wr_ref[...], h_top.astype(BF16)) + _dot_nt(wr_ref[...], (h2 - h_top).astype(BF16))
    logits = (both[:LANES] + both[LANES:] + br_ref[...])[:ROUTE_ROWS]
    row_i = lax.broadcasted_iota(jnp.int32, logits.shape, 0)
    row = row_i.astype(F32)
    big = jnp.float32(1e9)
    low = jnp.float32(-3e38)
    is_group = row_i < N_GROUPS
    gl = jnp.where(is_group, logits, low)
    gmax = jnp.max(gl, axis=0, keepdims=True)
    g_idx = jnp.min(jnp.where(gl == gmax, row, big), axis=0, keepdims=True)
    gsum = jnp.sum(jnp.where(is_group, jnp.exp(gl - gmax), 0.0), axis=0, keepdims=True)
    g_prob = 1.0 / gsum
    e_lo = N_GROUPS + g_idx * EXPERTS_PER_GROUP
    in_group = (row >= e_lo) & (row < e_lo + EXPERTS_PER_GROUP)
    el = jnp.where(in_group, logits, low)
    v1 = jnp.max(el, axis=0, keepdims=True)
    i1 = jnp.min(jnp.where(el == v1, row, big), axis=0, keepdims=True)
    el2 = jnp.where(row == i1, low, el)
    v2 = jnp.max(el2, axis=0, keepdims=True)
    i2 = jnp.min(jnp.where(el2 == v2, row, big), axis=0, keepdims=True)
    t = jnp.exp(v2 - v1)
    g0 = 1.0 / (1.0 + t) * g_prob
    g1 = t / (1.0 + t) * g_prob

    hit0 = row == i1
    hit1 = row == i2
    onehot = jnp.where(hit0, 1.0, jnp.where(hit1, 1.0, 0.0))
    r_i = lax.broadcasted_iota(jnp.int32, (mt, mt), 0)
    c_i = lax.broadcasted_iota(jnp.int32, (mt, mt), 1)
    earlier = jnp.where(r_i < c_i, 1.0, 0.0).astype(BF16)
    before = jnp.dot(onehot.astype(BF16), earlier, preferred_element_type=F32)
    cnt = jnp.sum(onehot, axis=1, keepdims=True)
    padded = jnp.floor((cnt + (CHUNK - 1)) * (1.0 / CHUNK)) * CHUNK
    a_i = lax.broadcasted_iota(jnp.int32, (ROUTE_ROWS, ROUTE_ROWS), 0)
    b_i = lax.broadcasted_iota(jnp.int32, (ROUTE_ROWS, ROUTE_ROWS), 1)
    lower = jnp.where(b_i < a_i, 1.0, 0.0)
    goff = jnp.dot(lower, jnp.broadcast_to(padded, (ROUTE_ROWS, LANES)), precision=HIGHEST,
                   preferred_element_type=F32)[:, 0:1]
    pos = before + goff
    slot0 = jnp.sum(jnp.where(hit0, pos, 0.0), axis=0, keepdims=True)
    slot1 = jnp.sum(jnp.where(hit1, pos, 0.0), axis=0, keepdims=True)
    return (i1 - N_GROUPS, i2 - N_GROUPS, g0, g1, slot0, slot1), cnt, goff


def _to_slabs(a):
    rows, d = a.shape
    g = a.reshape(rows // CHUNK, CHUNK, d)
    return jnp.concatenate([g[:, :, :d // 2], g[:, :, d // 2:]], axis=1).astype(BF16)


def _from_slabs(s):
    n, _, half = s.shape
    g = s.astype(F32)
    return (g[:, :CHUNK, :].reshape(n * CHUNK, half).astype(BF16), g[:, CHUNK:, :].reshape(n * CHUNK, half).astype(BF16))


def _post_mixer(o_bf, x_ref, mod_ref, g_ref, b_ref, wo_ref, wr_ref, br_ref, x1_ref, xs_ref, route_ref, meta_ref, alpha):
    y = jnp.dot(o_bf, wo_ref[...], preferred_element_type=F32)
    x1 = _layer_norm(alpha * x_ref[...] + mod_ref[2:3, :] * y, g_ref, b_ref)
    x1_ref[...] = x1
    h2 = x1 * (1.0 + mod_ref[4:5, :]) + mod_ref[3:4, :]
    rc = xs_ref.shape[0] * CHUNK
    mt = h2.shape[0]
    rows, cnt, goff = _route_tile(h2, wr_ref, br_ref, rc)

    s_i = lax.broadcasted_iota(jnp.int32, (rc, mt), 0).astype(F32)
    pick = jnp.where(s_i == rows[4], 1.0, jnp.where(s_i == rows[5], 1.0, 0.0)).astype(BF16)
    xs_ref[...] = _to_slabs(jnp.dot(pick, h2.astype(BF16), preferred_element_type=F32))

    sub = lax.broadcasted_iota(jnp.int32, (LANES, mt), 0)
    stacked = jnp.zeros((LANES, mt), F32)
    for j, r in enumerate(rows):
        stacked = jnp.where(sub == j, r, stacked)
    route_ref[...] = stacked.T[:, :ROUTE_W]
    lane = lax.broadcasted_iota(jnp.int32, (ROUTE_ROWS, LANES), 1)
    meta_ref[...] = jnp.where(lane == 0, cnt, jnp.where(lane == 1, goff, 0.0))


def _post_attn_kernel(ol_ref, oc_ref, x_ref, mod_ref, g_ref, b_ref, wo_ref, wr_ref, br_ref, x1_ref, xs_ref, route_ref, meta_ref,
                      *, alpha, n_lat_tiles, n_tiles):
    is_lat = jnp.minimum(pl.program_id(0), n_tiles - 1) < n_lat_tiles
    o = jnp.where(is_lat, ol_ref[...], oc_ref[...])
    _post_mixer(o, x_ref, mod_ref, g_ref, b_ref, wo_ref, wr_ref, br_ref, x1_ref, xs_ref, route_ref, meta_ref, alpha)


def _post_conv_kernel(gb_ref, z_ref, zp_ref, zn_ref, cw_ref, x_ref, mod_ref, g_ref, b_ref, wo_ref, wr_ref, br_ref,
                      x1_ref, xs_ref, route_ref, meta_ref, *, alpha, t_lat, seq, ctx_len, n_tiles):
    i = jnp.minimum(pl.program_id(0), n_tiles - 1)
    tm, d = z_ref.shape
    halo = zp_ref.shape[0]
    z = z_ref[...].astype(F32)
    row = lax.broadcasted_iota(jnp.int32, (tm, 1), 0)
    g = i * tm + row
    pos = jnp.where(g < t_lat, g % seq, g % ctx_len)
    n_seq = jnp.where(g < t_lat, seq, ctx_len)
    prev_row = zp_ref[halo - 1:halo, :].astype(F32)
    next_row = zn_ref[0:1, :].astype(F32)
    z_prev = jnp.where(row == 0, prev_row, pltpu.roll(z, 1, 0))
    z_next = jnp.where(row == tm - 1, next_row, pltpu.roll(z, tm - 1, 0))
    z_prev = jnp.where(pos == 0, 0.0, z_prev)
    z_next = jnp.where(pos == n_seq - 1, 0.0, z_next)
    y = cw_ref[0:1, :] * z_prev + cw_ref[1:2, :] * z + cw_ref[2:3, :] * z_next
    o = (gb_ref[...].astype(F32) * y).astype(BF16)
    _post_mixer(o, x_ref, mod_ref, g_ref, b_ref, wo_ref, wr_ref, br_ref, x1_ref, xs_ref, route_ref, meta_ref, alpha)


def _compact_rows(mt):
    return TOP_K * mt + N_EXPERTS * CHUNK


def _spare_tiles(rc, bm):
    return -(-2 * bm // rc)


def _post_mixer_call(kind, mixer_out, xs, mods, ln_g, ln_b, wo_bf, wr, br, dims, n_tok, alpha, conv_w=None):
    n_batch, seq, ctx_len = dims
    d = xs.shape[1]
    tm = MOE_TILE
    rc = _compact_rows(tm)
    n_tiles = n_tok // tm
    n_lat_tiles = n_batch * seq // tm
    tpb = seq // tm
    seg = functools.partial(_seg_of_tile, n_lat_tiles=n_lat_tiles, tiles_per_batch=tpb, n_batch=n_batch)
    n_spare = _spare_tiles(rc, EXPERT_BLOCK)
    tile = lambda i: jnp.minimum(i, n_tiles - 1)
    tok = pl.BlockSpec((tm, d), lambda i: (tile(i), 0))
    full = lambda a: pl.BlockSpec(a.shape, lambda i: (0,) * a.ndim)
    common_ins = (xs, mods, ln_g, ln_b, wo_bf, wr, br)
    common_specs = [tok, pl.BlockSpec((None, 6, d), lambda i: (seg(tile(i)), 0, 0)), full(ln_g), full(ln_b),
                    full(wo_bf), full(wr), full(br)]
    if kind == 1:
        gb, z = mixer_out
        halo = 16
        hb = tm // halo
        last_blk = z.shape[0] // halo - 1
        ins = (gb, z, z, z, conv_w) + common_ins
        in_specs = [tok, tok,
                    pl.BlockSpec((halo, d), lambda i: (jnp.maximum(tile(i) * hb - 1, 0), 0)),
                    pl.BlockSpec((halo, d), lambda i: (jnp.minimum((tile(i) + 1) * hb, last_blk), 0)),
                    full(conv_w)] + common_specs
        body = functools.partial(_post_conv_kernel, alpha=alpha, t_lat=n_batch * seq, seq=seq, ctx_len=ctx_len,
                                 n_tiles=n_tiles)
    else:
        o_lat, o_ctx = mixer_out
        n_ctx_tiles = o_ctx.shape[0] // tm
        ins = (o_lat, o_ctx) + common_ins
        in_specs = [pl.BlockSpec((tm, d), lambda i: (jnp.minimum(tile(i), n_lat_tiles - 1), 0)),
                    pl.BlockSpec((tm, d), lambda i: (jnp.clip(tile(i) - n_lat_tiles, 0, n_ctx_tiles - 1), 0))] + common_specs
        body = functools.partial(_post_attn_kernel, alpha=alpha, n_lat_tiles=n_lat_tiles, n_tiles=n_tiles)
    return pl.pallas_call(
        body,
        out_shape=[jax.ShapeDtypeStruct((n_tok, d), F32),
                   jax.ShapeDtypeStruct(((n_tiles + n_spare) * rc // CHUNK, 2 * CHUNK, d // 2), BF16),
                   jax.ShapeDtypeStruct((n_tok, ROUTE_W), F32),
                   jax.ShapeDtypeStruct((n_tiles, ROUTE_ROWS, LANES), F32)],
        grid=(n_tiles + n_spare,),
        in_specs=in_specs,
        out_specs=[tok, pl.BlockSpec((rc // CHUNK, 2 * CHUNK, d // 2), lambda i: (i, 0, 0)),
                   pl.BlockSpec((tm, ROUTE_W), lambda i: (tile(i), 0)),
                   pl.BlockSpec((None, ROUTE_ROWS, LANES), lambda i: (tile(i), 0, 0))],
        compiler_params=_cparams(1),
        name=f"post_mixer_{kind}",
    )(*ins)


def _expert_schedule(meta, rc, bm):
    n_tiles = meta.shape[0]
    cpb = bm // CHUNK
    cnt = meta[:, N_GROUPS:N_GROUPS + N_EXPERTS, 0].astype(jnp.int32)
    goff = meta[:, N_GROUPS:N_GROUPS + N_EXPERTS, 1].astype(jnp.int32)
    nch = ((cnt + CHUNK - 1) // CHUNK).T
    incl = jnp.cumsum(nch, axis=1)
    per_e = incl[:, -1]
    nbk = (per_e + cpb - 1) // cpb
    blk_start = jnp.cumsum(nbk) - nbk
    src0 = (jnp.arange(n_tiles, dtype=jnp.int32)[None, :] * rc + goff.T) // CHUNK
    n_blocks = (TOP_K * n_tiles * MOE_TILE + (CHUNK - 1) * N_EXPERTS * n_tiles) // CHUNK // cpb + N_EXPERTS
    blk = jnp.arange(n_blocks, dtype=jnp.int32)
    blk_e = jnp.clip(jnp.sum(blk_start[None, :] <= blk[:, None], axis=1) - 1, 0, N_EXPERTS - 1).astype(jnp.int32)
    onehot = (blk_e[:, None] == jnp.arange(N_EXPERTS)[None, :]).astype(F32)
    tables = jnp.concatenate([incl, incl - nch, src0, blk_start[:, None], per_e[:, None]], axis=1).astype(F32)
    per_blk = jnp.round(jnp.dot(onehot, tables, precision=HIGHEST)).astype(jnp.int32)
    incl_b, excl_b, src_b = (per_blk[:, None, i * n_tiles:(i + 1) * n_tiles] for i in range(3))
    start_b, total_b = per_blk[:, 3 * n_tiles], per_blk[:, 3 * n_tiles + 1]
    lq = ((blk - start_b) * cpb)[:, None] + jnp.arange(cpb, dtype=jnp.int32)[None, :]
    run = jnp.sum(incl_b <= lq[:, :, None], axis=2)
    pick = jnp.arange(n_tiles, dtype=jnp.int32)[None, None, :] == run[:, :, None]
    first = jnp.sum(jnp.where(pick, excl_b, 0), axis=2)
    src = jnp.sum(jnp.where(pick, src_b, 0), axis=2)
    chunk_src = jnp.where(lq < total_b[:, None], src + (lq - first), -1).astype(jnp.int32).reshape(-1)
    n_used = jnp.sum(nbk).astype(jnp.int32).reshape(1)
    return chunk_src, blk_e, n_used, n_blocks


def _expert_kernel(blk_e_ref, src_ref, n_used_ref, x_hbm, wg_ref, wu_ref, wd_ref, y_hbm, xbuf, ybuf, gsem, ssem,
                   wgu_bf, wd_bf, *, zero_slab, trash_slab):
    b = pl.program_id(0)
    nb = pl.num_programs(0)
    _, cpb, _, half = xbuf.shape
    ff = wg_ref.shape[1]
    n_used = n_used_ref[0]

    def gather(blk, slot, real):
        base = jnp.where(real, blk, 0) * cpb
        for j in range(cpb):
            v = jnp.where(real, src_ref[base + j], -1)
            pltpu.make_async_copy(x_hbm.at[pl.ds(jnp.where(v < 0, zero_slab, v), 1)], xbuf.at[slot, pl.ds(j, 1)],
                                  gsem.at[slot]).start()

    def scatter(blk, slot, real):
        base = jnp.where(real, blk, 0) * cpb
        for j in range(cpb):
            v = jnp.where(real, src_ref[base + j], -1)
            dst = jnp.where(v < 0, trash_slab + slot * cpb + j, v)
            pltpu.make_async_copy(ybuf.at[slot, pl.ds(j, 1)], y_hbm.at[pl.ds(dst, 1)], ssem.at[slot]).start()

    def wait_gather(slot):
        pltpu.make_async_copy(x_hbm.at[pl.ds(0, cpb)], xbuf.at[slot], gsem.at[slot]).wait()

    def wait_scatter(slot):
        pltpu.make_async_copy(ybuf.at[slot], y_hbm.at[pl.ds(0, cpb)], ssem.at[slot]).wait()

    in_use = b < n_used

    @pl.when((b == 0) & in_use)
    def _():
        ybuf[...] = jnp.zeros_like(ybuf)
        scatter(0, 0, False)
        scatter(0, 1, False)
        gather(0, 0, True)

    @pl.when(in_use & ((b == 0) | (blk_e_ref[b] != blk_e_ref[jnp.maximum(b - 1, 0)])))
    def _():
        wgu_bf[:, :ff] = wg_ref[...].astype(BF16)
        wgu_bf[:, ff:] = wu_ref[...].astype(BF16)
        wd_bf[...] = wd_ref[...].astype(BF16)

    @pl.when(in_use)
    def _():
        slot = b % 2
        wait_gather(slot)
        wait_scatter(slot)
        gather(b + 1, 1 - slot, b + 1 < n_used)
        for part in range(EXPERT_PARTS):
            slabs = pl.ds(part * (cpb // EXPERT_PARTS), cpb // EXPERT_PARTS)
            xa, xb = _from_slabs(xbuf[slot, slabs])
            gu = (jnp.dot(xa, wgu_bf[:half, :], preferred_element_type=F32)
                  + jnp.dot(xb, wgu_bf[half:, :], preferred_element_type=F32))
            gate = gu[:, :ff]
            act = gate / (1.0 + jnp.exp(-gate)) * gu[:, ff:]
            ybuf[slot, slabs] = _to_slabs(jnp.dot(act.astype(BF16), wd_bf[...], preferred_element_type=F32))
        scatter(b, slot, True)

    @pl.when((b == nb - 1) & (n_used > 0))
    def _():
        wait_gather(n_used % 2)
        wait_scatter(0)
        wait_scatter(1)


def _experts(xs_c, chunk_src, blk_e, n_used, n_blocks, w_gate, w_up, w_down, layer, rc, bm):
    d = 2 * xs_c.shape[2]
    ff = w_gate.shape[-1]
    n_tiles = xs_c.shape[0] * CHUNK // rc - _spare_tiles(rc, bm)
    assert rc - TRASH_ROWS >= TOP_K * MOE_TILE + N_EXPERTS * (CHUNK - 1) and bm % (EXPERT_PARTS * CHUNK) == 0
    return pl.pallas_call(
        functools.partial(_expert_kernel, zero_slab=(rc - TRASH_ROWS) // CHUNK, trash_slab=n_tiles * rc // CHUNK),
        out_shape=jax.ShapeDtypeStruct(xs_c.shape, xs_c.dtype),
        grid_spec=pltpu.PrefetchScalarGridSpec(
            num_scalar_prefetch=3,
            grid=(n_blocks,),
            in_specs=[
                pl.BlockSpec(memory_space=pl.ANY),
                pl.BlockSpec((None, None, d, ff), lambda b, be, cs, nu: (layer, be[b], 0, 0)),
                pl.BlockSpec((None, None, d, ff), lambda b, be, cs, nu: (layer, be[b], 0, 0)),
                pl.BlockSpec((None, None, ff, d), lambda b, be, cs, nu: (layer, be[b], 0, 0)),
            ],
            out_specs=pl.BlockSpec(memory_space=pl.ANY),
            scratch_shapes=[
                pltpu.VMEM((2, bm // CHUNK, 2 * CHUNK, d // 2), BF16),
                pltpu.VMEM((2, bm // CHUNK, 2 * CHUNK, d // 2), BF16),
                pltpu.SemaphoreType.DMA((2,)),
                pltpu.SemaphoreType.DMA((2,)),
                pltpu.VMEM((d, 2 * ff), BF16),
                pltpu.VMEM((ff, d), BF16),
            ],
        ),
        input_output_aliases={3: 0},
        compiler_params=_cparams(1),
        name="moe_experts",
    )(blk_e, chunk_src, n_used, xs_c, w_gate, w_up, w_down)


def _combine_kernel(ys_ref, x1_ref, route_ref, mod_ref, g_ref, b_ref, o_ref, *, alpha):
    rc = ys_ref.shape[0] * CHUNK
    mt = x1_ref.shape[0]
    ya, yb = _from_slabs(ys_ref[...])
    route = route_ref[...]
    s_i = lax.broadcasted_iota(jnp.int32, (mt, rc), 1).astype(F32)
    y = None
    for k in range(TOP_K):
        pick = jnp.where(s_i == route[:, 4 + k:5 + k], 1.0, 0.0).astype(BF16)
        yk = route[:, 2 + k:3 + k] * jnp.concatenate([jnp.dot(pick, ya, preferred_element_type=F32),
                                                      jnp.dot(pick, yb, preferred_element_type=F32)], axis=1)
        y = yk if y is None else y + yk
    o_ref[...] = _layer_norm(alpha * x1_ref[...] + mod_ref[5:6, :] * y, g_ref, b_ref)


def _combine(ys_c, x1, route, mods, ln_g, ln_b, dims, n_tok, alpha, rc):
    n_batch, seq, ctx_len = dims
    d = x1.shape[1]
    tg = MOE_TILE
    n_lat_tiles = n_batch * seq // tg
    tpb = seq // tg
    seg = functools.partial(_seg_of_tile, n_lat_tiles=n_lat_tiles, tiles_per_batch=tpb, n_batch=n_batch)
    return pl.pallas_call(
        functools.partial(_combine_kernel, alpha=alpha),
        out_shape=jax.ShapeDtypeStruct((n_tok, d), F32),
        grid=(n_tok // tg,),
        in_specs=[
            pl.BlockSpec((rc // CHUNK, 2 * CHUNK, d // 2), lambda i: (i, 0, 0)),
            pl.BlockSpec((tg, d), lambda i: (i, 0)),
            pl.BlockSpec((tg, ROUTE_W), lambda i: (i, 0)),
            pl.BlockSpec((None, 6, d), lambda i: (seg(i), 0, 0)),
            pl.BlockSpec(ln_g.shape, lambda i: (0, 0)),
            pl.BlockSpec(ln_b.shape, lambda i: (0, 0)),
        ],
        out_specs=pl.BlockSpec((tg, d), lambda i: (i, 0)),
        compiler_params=_cparams(1),
        name="moe_combine",
    )(ys_c, x1, route, mods, ln_g, ln_b)


def kernel(x, c, ctx, c_ctx, w_ada, b_ada, ln1_g, ln1_b, ln2_g, ln2_b, attn_w_qkv, attn_w_o, attn_sink, conv_w_in, conv_w,
           conv_w_out, nat_w_qkv, nat_w_o, nat_rpb, router_w_group, router_b_group, router_w_expert, router_b_expert,
           expert_w_gate, expert_w_up, expert_w_down):
    n_batch, seq, d = x.shape
    ctx_len = ctx.shape[1]
    depth = w_ada.shape[0]
    dims = (n_batch, seq, ctx_len)
    t_lat = n_batch * seq
    t_all = t_lat + n_batch * ctx_len
    alpha = float((2 * depth) ** 0.25)
    assert d == D_MODEL and n_batch + 1 <= ADA_ROWS
    assert seq % TOKEN_TILE == 0 and (n_batch * ctx_len) % TOKEN_TILE == 0 and seq % GRID_W == 0
    assert TOKEN_TILE % ctx_len == 0 or ctx_len % TOKEN_TILE == 0
    rc = _compact_rows(MOE_TILE)

    cc = jnp.zeros((ADA_ROWS, d), F32).at[:n_batch].set(c).at[n_batch].set(c_ctx)
    mods_all = _ada(cc, w_ada, b_ada).reshape(depth, ADA_ROWS, 6, d)
    rope_tabs = _rope_tables(seq, TOKEN_TILE)
    xs = jnp.concatenate([x.reshape(t_lat, d), ctx.reshape(n_batch * ctx_len, d)], axis=0)

    n_route_pad = LANES - N_GROUPS - N_EXPERTS
    for i in range(depth):
        last = i == depth - 1
        j = i // 3
        kind = i % 3
        mods = mods_all[i]
        n_tok = t_lat if last else t_all
        g1, b1 = ln1_g[i].reshape(1, d), ln1_b[i].reshape(1, d)
        g2, b2 = ln2_g[i].reshape(1, d), ln2_b[i].reshape(1, d)
        wr = jnp.concatenate([router_w_group[i], router_w_expert[i], jnp.zeros((d, n_route_pad), F32)], axis=1).T
        wr_top = _bf16_part(wr)
        wr = jnp.concatenate([wr_top.astype(BF16), (wr - wr_top).astype(BF16)], axis=0)
        br = jnp.concatenate([router_b_group[i], router_b_expert[i], jnp.zeros((n_route_pad,), F32)]).reshape(LANES, 1)

        if kind == 0:
            q, k, v = _proj_in(0, xs, mods, attn_w_qkv[j].astype(BF16), dims, rope_tabs)
            o = _window_attn(q, k, v, attn_sink[j], dims)
            o_ctx = o if last else _ctx_attn(q, k, v, attn_sink[j], dims, A_KV_HEADS, True)
            post = _post_mixer_call(0, (o, o_ctx), xs, mods, g1, b1, attn_w_o[j].astype(BF16), wr, br, dims, n_tok, alpha)
        elif kind == 1:
            gb, z = _proj_in(1, xs, mods, conv_w_in[j].astype(BF16), dims)
            post = _post_mixer_call(1, (gb, z), xs, mods, g1, b1, conv_w_out[j].astype(BF16), wr, br, dims, n_tok, alpha,
                                    conv_w=conv_w[j])
        else:
            q, k, v = _proj_in(2, xs, mods, nat_w_qkv[j].astype(BF16), dims)
            o = _nat_attn(q, k, v, nat_rpb[j], dims)
            o_ctx = o if last else _ctx_attn(q, k, v, jnp.zeros((N_HEADS,), F32), dims, N_HEADS, False)
            post = _post_mixer_call(2, (o, o_ctx), xs, mods, g1, b1, nat_w_o[j].astype(BF16), wr, br, dims, n_tok, alpha)

        x1, xs_c, route, meta = post
        chunk_src, blk_e, n_used, n_blocks = _expert_schedule(meta, rc, EXPERT_BLOCK)
        ys_c = _experts(xs_c, chunk_src, blk_e, n_used, n_blocks, expert_w_gate, expert_w_up, expert_w_down, i, rc,
                        EXPERT_BLOCK)
        xs = _combine(ys_c, x1, route, mods, g2, b2, dims, n_tok, alpha, rc)
    return xs[:t_lat].reshape(n_batch, seq, d)
```

```python
import functools

import jax
import jax.numpy as jnp
from jax import lax
from jax.experimental import pallas as pl
from jax.experimental.pallas import tpu as pltpu

F32 = jnp.float32
BF16 = jnp.bfloat16
HIGHEST = lax.Precision.HIGHEST

D_MODEL = 1024
HEAD_DIM = 64
N_HEADS = D_MODEL // HEAD_DIM
A_KV_HEADS = N_HEADS // 4
A_KV = A_KV_HEADS * HEAD_DIM
A_WINDOW = 128
A_BLOCK = 128
GRID_W = 64
ROPE_BASE = 10000.0
NA_KH_MAX = 8
NA_KW = 16
N_GROUPS = 4
EXPERTS_PER_GROUP = 8
N_EXPERTS = N_GROUPS * EXPERTS_PER_GROUP
TOP_K = 2
EXPERT_FF = D_MODEL // 2
LN_EPS = 1e-5
NEG_INF = -1e30
LOG2_E = 1.4426950408889634
Q_SCALE = HEAD_DIM ** -0.5 * LOG2_E

LANES = 128
ADA_ROWS = 24
ROUTE_W = 8
TOKEN_TILE = 512
MOE_TILE = 256
EXPERT_BLOCK = 512
EXPERT_PARTS = 2
CHUNK = 8
TRASH_ROWS = 32
ROUTE_ROWS = 64
NAT_ROWS = 4
NAT_WINDOW_ROWS = 12
VMEM_LIMIT = 56 * 1024 * 1024


def _cparams(n_axes=1):
    return pltpu.CompilerParams(dimension_semantics=("arbitrary",) * n_axes, vmem_limit_bytes=VMEM_LIMIT)


def _ada_kernel(c_ref, w_ref, b_ref, o_ref):
    cc = c_ref[...]
    s = cc / (1.0 + jnp.exp(-cc))
    o_ref[...] = jnp.dot(s, w_ref[...], precision=HIGHEST, preferred_element_type=F32) + b_ref[...]


def _ada(cc, w_ada, b_ada):
    depth, d, n = w_ada.shape
    nt = n // 4
    return pl.pallas_call(
        _ada_kernel,
        out_shape=jax.ShapeDtypeStruct((depth, ADA_ROWS, n), F32),
        grid=(depth, n // nt),
        in_specs=[
            pl.BlockSpec((ADA_ROWS, d), lambda i, j: (0, 0)),
            pl.BlockSpec((None, d, nt), lambda i, j: (i, 0, j)),
            pl.BlockSpec((None, 1, nt), lambda i, j: (i, 0, j)),
        ],
        out_specs=pl.BlockSpec((None, ADA_ROWS, nt), lambda i, j: (i, 0, j)),
        compiler_params=_cparams(2),
        name="ada_mod",
    )(cc, w_ada, b_ada.reshape(depth, 1, n))


def _modulated(x_ref, mod_ref, shift_row):
    return x_ref[...] * (1.0 + mod_ref[shift_row + 1:shift_row + 2, :]) + mod_ref[shift_row:shift_row + 1, :]


def _layer_norm(r, g_ref, b_ref):
    mu = jnp.mean(r, axis=-1, keepdims=True)
    rc = r - mu
    var = jnp.mean(rc * rc, axis=-1, keepdims=True)
    return rc * lax.rsqrt(var + LN_EPS) * g_ref[...] + b_ref[...]


def _dot_nt(a, b):
    return lax.dot_general(a, b, (((1,), (1,)), ((), ())), preferred_element_type=F32)


def _softmax_pv(s, v, sink_col=None):
    m = jnp.max(s, axis=-1, keepdims=True)
    if sink_col is not None:
        m = jnp.maximum(m, sink_col)
    e = jnp.exp2(s - m)
    den = jnp.sum(e, axis=-1, keepdims=True)
    if sink_col is not None:
        den = den + jnp.exp2(sink_col - m)
    return jnp.dot(e.astype(BF16), v, preferred_element_type=F32) / den


def _softmax_pv_ones(s, v_ones, nv):
    m = jnp.max(s, axis=-1, keepdims=True)
    both = jnp.dot(jnp.exp2(s - m).astype(BF16), v_ones, preferred_element_type=F32)
    return both[:, :nv] / both[:, nv:nv + 1]


def _seg_of_tile(i, n_lat_tiles, tiles_per_batch, n_batch):
    return jnp.where(i < n_lat_tiles, i // tiles_per_batch, n_batch)


class _TileOfTwo:
    def __init__(self, lat_ref, ctx_ref, is_lat):
        self.lat_ref, self.ctx_ref, self.is_lat = lat_ref, ctx_ref, is_lat

    def __getitem__(self, idx):
        return jnp.where(self.is_lat, self.lat_ref[idx], self.ctx_ref[idx])


def _stream_specs(xs, tm, tile):
    if not isinstance(xs, tuple):
        return (xs,), [pl.BlockSpec((tm, xs.shape[1]), lambda i: (tile(i), 0))]
    n_lat, n_ctx = xs[0].shape[0] // tm, xs[1].shape[0] // tm
    return xs, [pl.BlockSpec((tm, xs[0].shape[1]), lambda i: (jnp.minimum(tile(i), n_lat - 1), 0)),
                pl.BlockSpec((tm, xs[1].shape[1]), lambda i: (jnp.clip(tile(i) - n_lat, 0, n_ctx - 1), 0))]


def _with_stream(body, pos, xs, tm, tile):
    if not isinstance(xs, tuple):
        return body
    n_lat = xs[0].shape[0] // tm

    def wrapped(*refs):
        x = _TileOfTwo(refs[pos], refs[pos + 1], tile(pl.program_id(0)) < n_lat)
        return body(*refs[:pos], x, *refs[pos + 2:])
    return wrapped


def _proj_attn_kernel(x_ref, mod_ref, cos_ref, sin_ref, w_ref, q_ref, k_ref, v_ref):
    h = _modulated(x_ref, mod_ref, 0).astype(BF16)
    cos = cos_ref[...]
    sin = sin_ref[...]
    lane = lax.broadcasted_iota(jnp.int32, cos.shape, 1)
    low_half = (lane % 32) < 16

    def rope(a):
        partner = jnp.where(low_half, pltpu.roll(a, LANES - 16, 1), pltpu.roll(a, 16, 1))
        return a * cos + partner * sin

    def rope_cols(c0, width):
        a = jnp.dot(h, w_ref[:, c0:c0 + width], preferred_element_type=F32)
        return jnp.concatenate([rope(a[:, j * LANES:(j + 1) * LANES]) for j in range(width // LANES)], axis=1)

    nq = q_ref.shape[1]
    nk = k_ref.shape[1]
    wide = 2 * LANES
    for j in range(nq // wide):
        q_ref[:, j * wide:(j + 1) * wide] = (rope_cols(j * wide, wide) * Q_SCALE).astype(BF16)
    for j in range(nk // wide):
        k_ref[:, j * wide:(j + 1) * wide] = rope_cols(nq + j * wide, wide).astype(BF16)
    v_ref[...] = jnp.dot(h, w_ref[:, nq + nk:], preferred_element_type=F32).astype(BF16)


def _proj_nat_kernel(x_ref, mod_ref, w_ref, q_ref, k_ref, v_ref):
    h = _modulated(x_ref, mod_ref, 0).astype(BF16)
    d = q_ref.shape[1]
    q_ref[...] = (jnp.dot(h, w_ref[:, :d], preferred_element_type=F32) * Q_SCALE).astype(BF16)
    k_ref[...] = jnp.dot(h, w_ref[:, d:2 * d], preferred_element_type=F32).astype(BF16)
    v_ref[...] = jnp.dot(h, w_ref[:, 2 * d:], preferred_element_type=F32).astype(BF16)


def _proj_conv_kernel(x_ref, mod_ref, w_ref, gb_ref, z_ref):
    h = _modulated(x_ref, mod_ref, 0).astype(BF16)
    d = gb_ref.shape[1]
    gb_ref[...] = jnp.dot(h, w_ref[:, :d], preferred_element_type=F32).astype(BF16)
    gc = jnp.dot(h, w_ref[:, d:2 * d], preferred_element_type=F32)
    u = jnp.dot(h, w_ref[:, 2 * d:], preferred_element_type=F32)
    z_ref[...] = (gc * u).astype(BF16)


def _proj_in(kind, xs, mods, w_bf, dims, rope_tabs=None):
    n_batch, seq, ctx_len = dims
    t = sum(a.shape[0] for a in xs) if isinstance(xs, tuple) else xs.shape[0]
    d = w_bf.shape[0]
    tm = TOKEN_TILE
    n_lat_tiles = n_batch * seq // tm
    tpb = seq // tm
    seg = functools.partial(_seg_of_tile, n_lat_tiles=n_lat_tiles, tiles_per_batch=tpb, n_batch=n_batch)
    x_ins, x_specs = _stream_specs(xs, tm, lambda i: i)
    mod_spec = pl.BlockSpec((None, 6, d), lambda i: (seg(i), 0, 0))
    w_spec = pl.BlockSpec(w_bf.shape, lambda i: (0, 0))

    def tok_out(n):
        return jax.ShapeDtypeStruct((t, n), BF16), pl.BlockSpec((tm, n), lambda i: (i, 0))

    if kind == 0:
        cos, sin = rope_tabs
        tab_spec = pl.BlockSpec((tm, LANES), lambda i: (jnp.where(i < n_lat_tiles, i % tpb, tpb), 0))
        outs = [tok_out(d), tok_out(A_KV), tok_out(A_KV)]
        body, ins, in_specs = _proj_attn_kernel, (mods, cos, sin, w_bf), [mod_spec, tab_spec, tab_spec, w_spec]
    elif kind == 1:
        outs = [tok_out(d), tok_out(d)]
        body, ins, in_specs = _proj_conv_kernel, (mods, w_bf), [mod_spec, w_spec]
    else:
        outs = [tok_out(d), tok_out(d), tok_out(d)]
        body, ins, in_specs = _proj_nat_kernel, (mods, w_bf), [mod_spec, w_spec]
    return pl.pallas_call(
        _with_stream(body, 0, xs, tm, lambda i: i),
        out_shape=[o[0] for o in outs],
        grid=(t // tm,),
        in_specs=x_specs + in_specs,
        out_specs=[o[1] for o in outs],
        compiler_params=_cparams(1),
        name=f"proj_in_{kind}",
    )(*x_ins, *ins)


def _rope_tables(seq, tm):
    quarter = HEAD_DIM // 4
    inv_freq = ROPE_BASE ** (-jnp.arange(quarter, dtype=F32) / quarter)
    tpos = jnp.arange(seq)
    ang_r = (tpos // GRID_W).astype(F32)[:, None] * inv_freq
    ang_c = (tpos % GRID_W).astype(F32)[:, None] * inv_freq
    cos_h = jnp.concatenate([jnp.cos(ang_r), jnp.cos(ang_r), jnp.cos(ang_c), jnp.cos(ang_c)], axis=1)
    sin_h = jnp.concatenate([-jnp.sin(ang_r), jnp.sin(ang_r), -jnp.sin(ang_c), jnp.sin(ang_c)], axis=1)
    reps = LANES // HEAD_DIM
    cos = jnp.concatenate([jnp.tile(cos_h, (1, reps)), jnp.ones((tm, LANES), F32)], axis=0)
    sin = jnp.concatenate([jnp.tile(sin_h, (1, reps)), jnp.zeros((tm, LANES), F32)], axis=0)
    return cos, sin


def _window_attn_kernel(sink_ref, q_ref, kp_ref, kc_ref, kn_ref, kx_ref, vp_ref, vc_ref, vn_ref, vx_ref, o_ref, *, seq):
    qi = pl.program_id(1)
    blk = A_BLOCK
    group = N_HEADS // A_KV_HEADS
    kcat = jnp.concatenate([kp_ref[...], kc_ref[...], kn_ref[...], kx_ref[...]], axis=0)
    vcat = jnp.concatenate([vp_ref[...], vc_ref[...], vn_ref[...], vx_ref[...]], axis=0)
    rows = group * blk
    row = lax.broadcasted_iota(jnp.int32, (rows, blk), 0) & (blk - 1)
    col = lax.broadcasted_iota(jnp.int32, (rows, blk), 1)
    prev_ok = (col + (blk - A_WINDOW) >= row) & (qi > 0)
    next_ok = (col - (blk - A_WINDOW) <= row) & (qi < seq // blk - 1)
    for kk in range(A_KV_HEADS):
        k_h = kcat[:, kk * HEAD_DIM:(kk + 1) * HEAD_DIM]
        v_h = vcat[:, kk * HEAD_DIM:(kk + 1) * HEAD_DIM]
        heads = [kk * group + g for g in range(group)]
        q_st = jnp.concatenate([q_ref[:, h * HEAD_DIM:(h + 1) * HEAD_DIM] for h in heads], axis=0)
        sink_col = jnp.concatenate([jnp.full((blk, 1), sink_ref[h] * LOG2_E, F32) for h in heads], axis=0)
        s = _dot_nt(q_st, k_h)
        s = jnp.concatenate([jnp.where(prev_ok, s[:, :blk], NEG_INF), s[:, blk:2 * blk],
                             jnp.where(next_ok, s[:, 2 * blk:3 * blk], NEG_INF), s[:, 3 * blk:]], axis=1)
        o = _softmax_pv(s, v_h, sink_col).astype(o_ref.dtype)
        for g, h in enumerate(heads):
            o_ref[:, h * HEAD_DIM:(h + 1) * HEAD_DIM] = o[g * blk:(g + 1) * blk, :]


def _window_attn(q, k, v, sink, dims):
    n_batch, seq, ctx_len = dims
    t, d = q.shape
    blk = A_BLOCK
    nb = seq // blk
    t_lat = n_batch * seq
    ctx_blk0 = t_lat // ctx_len

    def q_map(b, i, s):
        return (b * nb + i, 0)

    def kv_map(off):
        return lambda b, i, s: (b * nb + jnp.clip(i + off, 0, nb - 1), 0)

    def ctx_map(b, i, s):
        return (ctx_blk0 + b, 0)

    kv_specs = [pl.BlockSpec((blk, A_KV), kv_map(-1)), pl.BlockSpec((blk, A_KV), kv_map(0)),
                pl.BlockSpec((blk, A_KV), kv_map(1)), pl.BlockSpec((ctx_len, A_KV), ctx_map)]
    return pl.pallas_call(
        functools.partial(_window_attn_kernel, seq=seq),
        out_shape=jax.ShapeDtypeStruct((t_lat, d), BF16),
        grid_spec=pltpu.PrefetchScalarGridSpec(
            num_scalar_prefetch=1,
            grid=(n_batch, nb),
            in_specs=[pl.BlockSpec((blk, d), q_map)] + kv_specs + kv_specs,
            out_specs=pl.BlockSpec((blk, d), q_map),
        ),
        compiler_params=_cparams(2),
        name="window_attn",
    )(sink, q, k, k, k, k, v, v, v, v)


def _ctx_attn_kernel(sink_ref, q_ref, k_ref, v_ref, o_ref, *, n_kv, use_sink):
    group = N_HEADS // n_kv
    n = q_ref.shape[0]
    for kk in range(n_kv):
        k_h = k_ref[:, kk * HEAD_DIM:(kk + 1) * HEAD_DIM]
        v_h = v_ref[:, kk * HEAD_DIM:(kk + 1) * HEAD_DIM]
        heads = [kk * group + g for g in range(group)]
        q_st = jnp.concatenate([q_ref[:, h * HEAD_DIM:(h + 1) * HEAD_DIM] for h in heads], axis=0)
        sink_col = None
        if use_sink:
            sink_col = jnp.concatenate([jnp.full((n, 1), sink_ref[h] * LOG2_E, F32) for h in heads], axis=0)
        o = _softmax_pv(_dot_nt(q_st, k_h), v_h, sink_col).astype(o_ref.dtype)
        for g, h in enumerate(heads):
            o_ref[:, h * HEAD_DIM:(h + 1) * HEAD_DIM] = o[g * n:(g + 1) * n, :]


def _ctx_attn(q, k, v, sink, dims, n_kv, use_sink):
    n_batch, seq, ctx_len = dims
    d = q.shape[1]
    kvw = n_kv * HEAD_DIM
    blk0 = n_batch * seq // ctx_len
    return pl.pallas_call(
        functools.partial(_ctx_attn_kernel, n_kv=n_kv, use_sink=use_sink),
        out_shape=jax.ShapeDtypeStruct((n_batch * ctx_len, d), BF16),
        grid_spec=pltpu.PrefetchScalarGridSpec(
            num_scalar_prefetch=1,
            grid=(n_batch,),
            in_specs=[pl.BlockSpec((ctx_len, d), lambda b, s: (blk0 + b, 0)),
                      pl.BlockSpec((ctx_len, kvw), lambda b, s: (blk0 + b, 0)),
                      pl.BlockSpec((ctx_len, kvw), lambda b, s: (blk0 + b, 0))],
            out_specs=pl.BlockSpec((ctx_len, d), lambda b, s: (b, 0)),
        ),
        compiler_params=_cparams(1),
        name="ctx_attn",
    )(sink, q, k, v)


def _nat_kernel(q_ref, k0_ref, k1_ref, k2_ref, kx_ref, v0_ref, v1_ref, v2_ref, vx_ref, bias_ref, o_ref):
    nq = q_ref.shape[0]
    low = lax.broadcasted_iota(jnp.int32, (nq, LANES), 1) < HEAD_DIM
    for a in range(N_HEADS // 2):
        sl = slice(a * LANES, (a + 1) * LANES)
        qp = q_ref[:, sl]
        k_loc = jnp.concatenate([k0_ref[:, sl], k1_ref[:, sl], k2_ref[:, sl]], axis=0)
        k_ctx = kx_ref[:, sl]
        v_all = jnp.concatenate([v0_ref[:, sl], v1_ref[:, sl], v2_ref[:, sl], vx_ref[:, sl]], axis=0)
        v_all = jnp.concatenate([v_all, jnp.ones_like(v_all)], axis=1)
        outs = []
        for hh in range(2):
            qm = jnp.where(low if hh == 0 else jnp.logical_not(low), qp, jnp.zeros_like(qp))
            s = jnp.concatenate([_dot_nt(qm, k_loc) + bias_ref[2 * a + hh], _dot_nt(qm, k_ctx)], axis=-1)
            outs.append(_softmax_pv_ones(s, v_all, LANES))
        o_ref[:, sl] = jnp.where(low, outs[0], outs[1]).astype(o_ref.dtype)


def _nat_bias_table(rpb, kh):
    g, wr = NAT_ROWS, NAT_WINDOW_ROWS
    n_dr, n_dc = 2 * NA_KH_MAX - 1, 2 * NA_KW - 1
    table = jnp.full((N_HEADS, n_dr + 1, n_dc + 1), NEG_INF, F32).at[:, :n_dr, :n_dc].set(rpb.astype(F32) * LOG2_E)
    qc = jnp.arange(GRID_W)
    kc = jnp.arange(GRID_W)
    q_start = jnp.clip(qc - NA_KW // 2, 0, GRID_W - NA_KW)
    col_ok = (kc[None, :] >= q_start[:, None]) & (kc[None, :] < q_start[:, None] + NA_KW)
    dc = jnp.where(col_ok, jnp.clip(kc[None, :] - qc[:, None] + NA_KW - 1, 0, n_dc - 1), n_dc)
    dc_hot = (dc[:, :, None] == jnp.arange(n_dc + 1)).astype(F32)
    by_col = jnp.einsum('hac,qkc->haqk', table, dc_hot, precision=HIGHEST)
    j = jnp.arange(g)
    i = jnp.arange(wr)
    tables = []
    for q_rel, rs_rel in ((j, 0 * j), (g + j, j), (wr - g + j, 0 * j + wr - kh)):
        row_ok = (i[None, :] >= rs_rel[:, None]) & (i[None, :] < rs_rel[:, None] + kh)
        dr = jnp.where(row_ok, jnp.clip(i[None, :] - q_rel[:, None] + NA_KH_MAX - 1, 0, n_dr - 1), n_dr)
        dr_hot = (dr[:, :, None] == jnp.arange(n_dr + 1)).astype(F32)
        b = jnp.einsum('jia,haqk->hjqik', dr_hot, by_col, precision=HIGHEST)
        tables.append(b.reshape(N_HEADS, g * GRID_W, wr * GRID_W))
    return jnp.stack(tables)


def _nat_attn(q, k, v, rpb, dims):
    n_batch, seq, ctx_len = dims
    t, d = q.shape
    n_rows = seq // GRID_W
    kh = min(NA_KH_MAX, n_rows)
    g, wr = NAT_ROWS, NAT_WINDOW_ROWS
    n_groups = n_rows // g
    assert kh == NA_KH_MAX and wr == 3 * g and wr >= kh + g - 1 and n_rows % g == 0 and n_groups >= 3
    bias = _nat_bias_table(rpb, kh)
    blk = g * GRID_W
    bpb = seq // blk
    ctx_blk0 = n_batch * seq // ctx_len

    def case_of(m):
        return jnp.where(m == 0, 0, jnp.where(m == n_groups - 1, 2, 1))

    def win(off):
        return lambda m, b: (b * bpb + jnp.clip(m - 1, 0, n_groups - 3) + off, 0)

    tok = pl.BlockSpec((blk, d), lambda m, b: (b * bpb + m, 0))
    ctx = pl.BlockSpec((ctx_len, d), lambda m, b: (ctx_blk0 + b, 0))
    kv = [pl.BlockSpec((blk, d), win(0)), pl.BlockSpec((blk, d), win(1)), pl.BlockSpec((blk, d), win(2)), ctx]
    return pl.pallas_call(
        _nat_kernel,
        out_shape=jax.ShapeDtypeStruct((n_batch * seq, d), BF16),
        grid=(n_groups, n_batch),
        in_specs=[tok] + kv + kv + [
            pl.BlockSpec((None, N_HEADS, blk, wr * GRID_W), lambda m, b: (case_of(m), 0, 0, 0))],
        out_specs=tok,
        compiler_params=_cparams(2),
        name="nat_attn",
    )(q, k, k, k, k, v, v, v, v, bias)


def _bf16_part(a):
    return lax.bitcast_convert_type(lax.bitcast_convert_type(a, jnp.uint32) & jnp.uint32(0xFFFF0000), F32)


def _route_tile(h2, wr_ref, br_ref, rc):
    mt = h2.shape[0]
    h_top = _bf16_part(h2)
    both = _dot_nt(wr_ref[...], h_top.astype(BF16)) + _dot_nt(wr_ref[...], (h2 - h_top).astype(BF16))
    logits = (both[:LANES] + both[LANES:] + br_ref[...])[:ROUTE_ROWS]
    row_i = lax.broadcasted_iota(jnp.int32, logits.shape, 0)
    row = row_i.astype(F32)
    big = jnp.float32(1e9)
    low = jnp.float32(-3e38)
    is_group = row_i < N_GROUPS
    gl = jnp.where(is_group, logits, low)
    gmax = jnp.max(gl, axis=0, keepdims=True)
    g_idx = jnp.min(jnp.where(gl == gmax, row, big), axis=0, keepdims=True)
    gsum = jnp.sum(jnp.where(is_group, jnp.exp(gl - gmax), 0.0), axis=0, keepdims=True)
    g_prob = 1.0 / gsum
    e_lo = N_GROUPS + g_idx * EXPERTS_PER_GROUP
    in_group = (row >= e_lo) & (row < e_lo + EXPERTS_PER_GROUP)
    el = jnp.where(in_group, logits, low)
    v1 = jnp.max(el, axis=0, keepdims=True)
    i1 = jnp.min(jnp.where(el == v1, row, big), axis=0, keepdims=True)
    el2 = jnp.where(row == i1, low, el)
    v2 = jnp.max(el2, axis=0, keepdims=True)
    i2 = jnp.min(jnp.where(el2 == v2, row, big), axis=0, keepdims=True)
    t = jnp.exp(v2 - v1)
    g0 = 1.0 / (1.0 + t) * g_prob
    g1 = t / (1.0 + t) * g_prob

    hit0 = row == i1
    hit1 = row == i2
    onehot = jnp.where(hit0, 1.0, jnp.where(hit1, 1.0, 0.0))
    r_i = lax.broadcasted_iota(jnp.int32, (mt, mt), 0)
    c_i = lax.broadcasted_iota(jnp.int32, (mt, mt), 1)
    earlier = jnp.where(r_i < c_i, 1.0, 0.0).astype(BF16)
    before = jnp.dot(onehot.astype(BF16), earlier, preferred_element_type=F32)
    cnt = jnp.sum(onehot, axis=1, keepdims=True)
    padded = jnp.floor((cnt + (CHUNK - 1)) * (1.0 / CHUNK)) * CHUNK
    a_i = lax.broadcasted_iota(jnp.int32, (ROUTE_ROWS, ROUTE_ROWS), 0)
    b_i = lax.broadcasted_iota(jnp.int32, (ROUTE_ROWS, ROUTE_ROWS), 1)
    lower = jnp.where(b_i < a_i, 1.0, 0.0)
    goff = jnp.dot(lower, jnp.broadcast_to(padded, (ROUTE_ROWS, LANES)), precision=HIGHEST,
                   preferred_element_type=F32)[:, 0:1]
    pos = before + goff
    slot0 = jnp.sum(jnp.where(hit0, pos, 0.0), axis=0, keepdims=True)
    slot1 = jnp.sum(jnp.where(hit1, pos, 0.0), axis=0, keepdims=True)
    return (i1 - N_GROUPS, i2 - N_GROUPS, g0, g1, slot0, slot1), cnt, goff


def _to_slabs(a):
    rows, d = a.shape
    g = a.reshape(rows // CHUNK, CHUNK, d)
    return jnp.concatenate([g[:, :, :d // 2], g[:, :, d // 2:]], axis=1).astype(BF16)


def _from_slabs(s):
    n, _, half = s.shape
    g = s.astype(F32)
    return (g[:, :CHUNK, :].reshape(n * CHUNK, half).astype(BF16), g[:, CHUNK:, :].reshape(n * CHUNK, half).astype(BF16))


def _post_mixer(o_bf, x_ref, mod_ref, g_ref, b_ref, wo_ref, wr_ref, br_ref, x1_ref, xs_ref, route_ref, meta_ref, alpha):
    y = jnp.dot(o_bf, wo_ref[...], preferred_element_type=F32)
    x1 = _layer_norm(alpha * x_ref[...] + mod_ref[2:3, :] * y, g_ref, b_ref)
    x1_ref[...] = x1
    h2 = x1 * (1.0 + mod_ref[4:5, :]) + mod_ref[3:4, :]
    rc = xs_ref.shape[0] * CHUNK
    mt = h2.shape[0]
    rows, cnt, goff = _route_tile(h2, wr_ref, br_ref, rc)

    s_i = lax.broadcasted_iota(jnp.int32, (rc, mt), 0).astype(F32)
    pick = jnp.where(s_i == rows[4], 1.0, jnp.where(s_i == rows[5], 1.0, 0.0)).astype(BF16)
    xs_ref[...] = _to_slabs(jnp.dot(pick, h2.astype(BF16), preferred_element_type=F32))

    sub = lax.broadcasted_iota(jnp.int32, (LANES, mt), 0)
    stacked = jnp.zeros((LANES, mt), F32)
    for j, r in enumerate(rows):
        stacked = jnp.where(sub == j, r, stacked)
    route_ref[...] = stacked.T[:, :ROUTE_W]
    lane = lax.broadcasted_iota(jnp.int32, (ROUTE_ROWS, LANES), 1)
    meta_ref[...] = jnp.where(lane == 0, cnt, jnp.where(lane == 1, goff, 0.0))


def _post_attn_kernel(ol_ref, oc_ref, x_ref, mod_ref, g_ref, b_ref, wo_ref, wr_ref, br_ref, x1_ref, xs_ref, route_ref, meta_ref,
                      *, alpha, n_lat_tiles, n_tiles):
    is_lat = jnp.minimum(pl.program_id(0), n_tiles - 1) < n_lat_tiles
    o = jnp.where(is_lat, ol_ref[...], oc_ref[...])
    _post_mixer(o, x_ref, mod_ref, g_ref, b_ref, wo_ref, wr_ref, br_ref, x1_ref, xs_ref, route_ref, meta_ref, alpha)


def _post_conv_kernel(gb_ref, z_ref, zp_ref, zn_ref, cw_ref, x_ref, mod_ref, g_ref, b_ref, wo_ref, wr_ref, br_ref,
                      x1_ref, xs_ref, route_ref, meta_ref, *, alpha, t_lat, seq, ctx_len, n_tiles):
    i = jnp.minimum(pl.program_id(0), n_tiles - 1)
    tm, d = z_ref.shape
    halo = zp_ref.shape[0]
    z = z_ref[...].astype(F32)
    row = lax.broadcasted_iota(jnp.int32, (tm, 1), 0)
    g = i * tm + row
    pos = jnp.where(g < t_lat, g % seq, g % ctx_len)
    n_seq = jnp.where(g < t_lat, seq, ctx_len)
    prev_row = zp_ref[halo - 1:halo, :].astype(F32)
    next_row = zn_ref[0:1, :].astype(F32)
    z_prev = jnp.where(row == 0, prev_row, pltpu.roll(z, 1, 0))
    z_next = jnp.where(row == tm - 1, next_row, pltpu.roll(z, tm - 1, 0))
    z_prev = jnp.where(pos == 0, 0.0, z_prev)
    z_next = jnp.where(pos == n_seq - 1, 0.0, z_next)
    y = cw_ref[0:1, :] * z_prev + cw_ref[1:2, :] * z + cw_ref[2:3, :] * z_next
    o = (gb_ref[...].astype(F32) * y).astype(BF16)
    _post_mixer(o, x_ref, mod_ref, g_ref, b_ref, wo_ref, wr_ref, br_ref, x1_ref, xs_ref, route_ref, meta_ref, alpha)


def _compact_rows(mt):
    return TOP_K * mt + N_EXPERTS * CHUNK


def _spare_tiles(rc, bm):
    return -(-2 * bm // rc)


def _post_mixer_call(kind, mixer_out, xs, mods, ln_g, ln_b, wo_bf, wr, br, dims, n_tok, alpha, conv_w=None):
    n_batch, seq, ctx_len = dims
    d = wo_bf.shape[1]
    tm = MOE_TILE
    rc = _compact_rows(tm)
    n_tiles = n_tok // tm
    n_lat_tiles = n_batch * seq // tm
    tpb = seq // tm
    seg = functools.partial(_seg_of_tile, n_lat_tiles=n_lat_tiles, tiles_per_batch=tpb, n_batch=n_batch)
    n_spare = _spare_tiles(rc, EXPERT_BLOCK)
    tile = lambda i: jnp.minimum(i, n_tiles - 1)
    tok = pl.BlockSpec((tm, d), lambda i: (tile(i), 0))
    full = lambda a: pl.BlockSpec(a.shape, lambda i: (0,) * a.ndim)
    x_ins, x_specs = _stream_specs(xs, tm, tile)
    common_ins = x_ins + (mods, ln_g, ln_b, wo_bf, wr, br)
    common_specs = x_specs + [pl.BlockSpec((None, 6, d), lambda i: (seg(tile(i)), 0, 0)), full(ln_g), full(ln_b),
                              full(wo_bf), full(wr), full(br)]
    if kind == 1:
        gb, z = mixer_out
        halo = 16
        hb = tm // halo
        last_blk = z.shape[0] // halo - 1
        ins = (gb, z, z, z, conv_w) + common_ins
        in_specs = [tok, tok,
                    pl.BlockSpec((halo, d), lambda i: (jnp.maximum(tile(i) * hb - 1, 0), 0)),
                    pl.BlockSpec((halo, d), lambda i: (jnp.minimum((tile(i) + 1) * hb, last_blk), 0)),
                    full(conv_w)] + common_specs
        body = _with_stream(functools.partial(_post_conv_kernel, alpha=alpha, t_lat=n_batch * seq, seq=seq,
                                              ctx_len=ctx_len, n_tiles=n_tiles), 5, xs, tm, tile)
    else:
        o_lat, o_ctx = mixer_out
        n_ctx_tiles = o_ctx.shape[0] // tm
        ins = (o_lat, o_ctx) + common_ins
        in_specs = [pl.BlockSpec((tm, d), lambda i: (jnp.minimum(tile(i), n_lat_tiles - 1), 0)),
                    pl.BlockSpec((tm, d), lambda i: (jnp.clip(tile(i) - n_lat_tiles, 0, n_ctx_tiles - 1), 0))] + common_specs
        body = _with_stream(functools.partial(_post_attn_kernel, alpha=alpha, n_lat_tiles=n_lat_tiles, n_tiles=n_tiles),
                            2, xs, tm, tile)
    return pl.pallas_call(
        body,
        out_shape=[jax.ShapeDtypeStruct((n_tok, d), F32),
                   jax.ShapeDtypeStruct(((n_tiles + n_spare) * rc // CHUNK, 2 * CHUNK, d // 2), BF16),
                   jax.ShapeDtypeStruct((n_tok, ROUTE_W), F32),
                   jax.ShapeDtypeStruct((n_tiles, ROUTE_ROWS, LANES), F32)],
        grid=(n_tiles + n_spare,),
        in_specs=in_specs,
        out_specs=[tok, pl.BlockSpec((rc // CHUNK, 2 * CHUNK, d // 2), lambda i: (i, 0, 0)),
                   pl.BlockSpec((tm, ROUTE_W), lambda i: (tile(i), 0)),
                   pl.BlockSpec((None, ROUTE_ROWS, LANES), lambda i: (tile(i), 0, 0))],
        compiler_params=_cparams(1),
        name=f"post_mixer_{kind}",
    )(*ins)


def _expert_schedule(meta, rc, bm):
    n_tiles = meta.shape[0]
    cpb = bm // CHUNK
    cnt = meta[:, N_GROUPS:N_GROUPS + N_EXPERTS, 0].astype(jnp.int32)
    goff = meta[:, N_GROUPS:N_GROUPS + N_EXPERTS, 1].astype(jnp.int32)
    nch = ((cnt + CHUNK - 1) // CHUNK).T
    incl = jnp.cumsum(nch, axis=1)
    per_e = incl[:, -1]
    nbk = (per_e + cpb - 1) // cpb
    blk_start = jnp.cumsum(nbk) - nbk
    src0 = (jnp.arange(n_tiles, dtype=jnp.int32)[None, :] * rc + goff.T) // CHUNK
    n_blocks = (TOP_K * n_tiles * MOE_TILE + (CHUNK - 1) * N_EXPERTS * n_tiles) // CHUNK // cpb + N_EXPERTS
    blk = jnp.arange(n_blocks, dtype=jnp.int32)
    blk_e = jnp.clip(jnp.sum(blk_start[None, :] <= blk[:, None], axis=1) - 1, 0, N_EXPERTS - 1).astype(jnp.int32)
    onehot = (blk_e[:, None] == jnp.arange(N_EXPERTS)[None, :]).astype(F32)
    tables = jnp.concatenate([incl, incl - nch, src0, blk_start[:, None], per_e[:, None]], axis=1).astype(F32)
    per_blk = jnp.round(jnp.dot(onehot, tables, precision=HIGHEST)).astype(jnp.int32)
    incl_b, excl_b, src_b = (per_blk[:, None, i * n_tiles:(i + 1) * n_tiles] for i in range(3))
    start_b, total_b = per_blk[:, 3 * n_tiles], per_blk[:, 3 * n_tiles + 1]
    lq = ((blk - start_b) * cpb)[:, None] + jnp.arange(cpb, dtype=jnp.int32)[None, :]
    run = jnp.sum(incl_b <= lq[:, :, None], axis=2)
    pick = jnp.arange(n_tiles, dtype=jnp.int32)[None, None, :] == run[:, :, None]
    first = jnp.sum(jnp.where(pick, excl_b, 0), axis=2)
    src = jnp.sum(jnp.where(pick, src_b, 0), axis=2)
    chunk_src = jnp.where(lq < total_b[:, None], src + (lq - first), -1).astype(jnp.int32).reshape(-1)
    n_used = jnp.sum(nbk).astype(jnp.int32).reshape(1)
    return chunk_src, blk_e, n_used, n_blocks


def _expert_kernel(blk_e_ref, src_ref, n_used_ref, x_hbm, wg_ref, wu_ref, wd_ref, y_hbm, xbuf, ybuf, gsem, ssem,
                   wgu_bf, wd_bf, *, zero_slab, trash_slab):
    b = pl.program_id(0)
    nb = pl.num_programs(0)
    _, cpb, _, half = xbuf.shape
    ff = wg_ref.shape[1]
    n_used = n_used_ref[0]

    def gather(blk, slot, real):
        base = jnp.where(real, blk, 0) * cpb
        for j in range(cpb):
            v = jnp.where(real, src_ref[base + j], -1)
            pltpu.make_async_copy(x_hbm.at[pl.ds(jnp.where(v < 0, zero_slab, v), 1)], xbuf.at[slot, pl.ds(j, 1)],
                                  gsem.at[slot]).start()

    def scatter(blk, slot, real):
        base = jnp.where(real, blk, 0) * cpb
        for j in range(cpb):
            v = jnp.where(real, src_ref[base + j], -1)
            dst = jnp.where(v < 0, trash_slab + slot * cpb + j, v)
            pltpu.make_async_copy(ybuf.at[slot, pl.ds(j, 1)], y_hbm.at[pl.ds(dst, 1)], ssem.at[slot]).start()

    def wait_gather(slot):
        pltpu.make_async_copy(x_hbm.at[pl.ds(0, cpb)], xbuf.at[slot], gsem.at[slot]).wait()

    def wait_scatter(slot):
        pltpu.make_async_copy(ybuf.at[slot], y_hbm.at[pl.ds(0, cpb)], ssem.at[slot]).wait()

    in_use = b < n_used

    @pl.when((b == 0) & in_use)
    def _():
        ybuf[...] = jnp.zeros_like(ybuf)
        scatter(0, 0, False)
        scatter(0, 1, False)
        gather(0, 0, True)

    @pl.when(in_use & ((b == 0) | (blk_e_ref[b] != blk_e_ref[jnp.maximum(b - 1, 0)])))
    def _():
        wgu_bf[:, :ff] = wg_ref[...].astype(BF16)
        wgu_bf[:, ff:] = wu_ref[...].astype(BF16)
        wd_bf[...] = wd_ref[...].astype(BF16)

    @pl.when(in_use)
    def _():
        slot = b % 2
        wait_gather(slot)
        wait_scatter(slot)
        gather(b + 1, 1 - slot, b + 1 < n_used)
        for part in range(EXPERT_PARTS):
            slabs = pl.ds(part * (cpb // EXPERT_PARTS), cpb // EXPERT_PARTS)
            xa, xb = _from_slabs(xbuf[slot, slabs])
            gu = (jnp.dot(xa, wgu_bf[:half, :], preferred_element_type=F32)
                  + jnp.dot(xb, wgu_bf[half:, :], preferred_element_type=F32))
            gate = gu[:, :ff]
            act = gate / (1.0 + jnp.exp(-gate)) * gu[:, ff:]
            ybuf[slot, slabs] = _to_slabs(jnp.dot(act.astype(BF16), wd_bf[...], preferred_element_type=F32))
        scatter(b, slot, True)

    @pl.when((b == nb - 1) & (n_used > 0))
    def _():
        wait_gather(n_used % 2)
        wait_scatter(0)
        wait_scatter(1)


def _experts(xs_c, chunk_src, blk_e, n_used, n_blocks, w_gate, w_up, w_down, layer, rc, bm):
    d = 2 * xs_c.shape[2]
    ff = w_gate.shape[-1]
    n_tiles = xs_c.shape[0] * CHUNK // rc - _spare_tiles(rc, bm)
    assert rc - TRASH_ROWS >= TOP_K * MOE_TILE + N_EXPERTS * (CHUNK - 1) and bm % (EXPERT_PARTS * CHUNK) == 0
    return pl.pallas_call(
        functools.partial(_expert_kernel, zero_slab=(rc - TRASH_ROWS) // CHUNK, trash_slab=n_tiles * rc // CHUNK),
        out_shape=jax.ShapeDtypeStruct(xs_c.shape, xs_c.dtype),
        grid_spec=pltpu.PrefetchScalarGridSpec(
            num_scalar_prefetch=3,
            grid=(n_blocks,),
            in_specs=[
                pl.BlockSpec(memory_space=pl.ANY),
                pl.BlockSpec((None, None, d, ff), lambda b, be, cs, nu: (layer, be[b], 0, 0)),
                pl.BlockSpec((None, None, d, ff), lambda b, be, cs, nu: (layer, be[b], 0, 0)),
                pl.BlockSpec((None, None, ff, d), lambda b, be, cs, nu: (layer, be[b], 0, 0)),
            ],
            out_specs=pl.BlockSpec(memory_space=pl.ANY),
            scratch_shapes=[
                pltpu.VMEM((2, bm // CHUNK, 2 * CHUNK, d // 2), BF16),
                pltpu.VMEM((2, bm // CHUNK, 2 * CHUNK, d // 2), BF16),
                pltpu.SemaphoreType.DMA((2,)),
                pltpu.SemaphoreType.DMA((2,)),
                pltpu.VMEM((d, 2 * ff), BF16),
                pltpu.VMEM((ff, d), BF16),
            ],
        ),
        input_output_aliases={3: 0},
        compiler_params=_cparams(1),
        name="moe_experts",
    )(blk_e, chunk_src, n_used, xs_c, w_gate, w_up, w_down)


def _combine_kernel(ys_ref, x1_ref, route_ref, mod_ref, g_ref, b_ref, o_ref, *, alpha):
    rc = ys_ref.shape[0] * CHUNK
    mt = x1_ref.shape[0]
    ya, yb = _from_slabs(ys_ref[...])
    route = route_ref[...]
    s_i = lax.broadcasted_iota(jnp.int32, (mt, rc), 1).astype(F32)
    y = None
    for k in range(TOP_K):
        pick = jnp.where(s_i == route[:, 4 + k:5 + k], 1.0, 0.0).astype(BF16)
        yk = route[:, 2 + k:3 + k] * jnp.concatenate([jnp.dot(pick, ya, preferred_element_type=F32),
                                                      jnp.dot(pick, yb, preferred_element_type=F32)], axis=1)
        y = yk if y is None else y + yk
    o_ref[...] = _layer_norm(alpha * x1_ref[...] + mod_ref[5:6, :] * y, g_ref, b_ref)


def _combine(ys_c, x1, route, mods, ln_g, ln_b, dims, n_tok, alpha, rc):
    n_batch, seq, ctx_len = dims
    d = x1.shape[1]
    tg = MOE_TILE
    n_lat_tiles = n_batch * seq // tg
    tpb = seq // tg
    seg = functools.partial(_seg_of_tile, n_lat_tiles=n_lat_tiles, tiles_per_batch=tpb, n_batch=n_batch)
    return pl.pallas_call(
        functools.partial(_combine_kernel, alpha=alpha),
        out_shape=jax.ShapeDtypeStruct((n_tok, d), F32),
        grid=(n_tok // tg,),
        in_specs=[
            pl.BlockSpec((rc // CHUNK, 2 * CHUNK, d // 2), lambda i: (i, 0, 0)),
            pl.BlockSpec((tg, d), lambda i: (i, 0)),
            pl.BlockSpec((tg, ROUTE_W), lambda i: (i, 0)),
            pl.BlockSpec((None, 6, d), lambda i: (seg(i), 0, 0)),
            pl.BlockSpec(ln_g.shape, lambda i: (0, 0)),
            pl.BlockSpec(ln_b.shape, lambda i: (0, 0)),
        ],
        out_specs=pl.BlockSpec((tg, d), lambda i: (i, 0)),
        compiler_params=_cparams(1),
        name="moe_combine",
    )(ys_c, x1, route, mods, ln_g, ln_b)


def kernel(x, c, ctx, c_ctx, w_ada, b_ada, ln1_g, ln1_b, ln2_g, ln2_b, attn_w_qkv, attn_w_o, attn_sink, conv_w_in, conv_w,
           conv_w_out, nat_w_qkv, nat_w_o, nat_rpb, router_w_group, router_b_group, router_w_expert, router_b_expert,
           expert_w_gate, expert_w_up, expert_w_down):
    n_batch, seq, d = x.shape
    ctx_len = ctx.shape[1]
    depth = w_ada.shape[0]
    dims = (n_batch, seq, ctx_len)
    t_lat = n_batch * seq
    t_all = t_lat + n_batch * ctx_len
    alpha = float((2 * depth) ** 0.25)
    assert d == D_MODEL and n_batch + 1 <= ADA_ROWS
    assert seq % TOKEN_TILE == 0 and (n_batch * ctx_len) % TOKEN_TILE == 0 and seq % GRID_W == 0
    assert TOKEN_TILE % ctx_len == 0 or ctx_len % TOKEN_TILE == 0
    rc = _compact_rows(MOE_TILE)

    cc = jnp.zeros((ADA_ROWS, d), F32).at[:n_batch].set(c).at[n_batch].set(c_ctx)
    mods_all = _ada(cc, w_ada, b_ada).reshape(depth, ADA_ROWS, 6, d)
    rope_tabs = _rope_tables(seq, TOKEN_TILE)
    xs = (x.reshape(t_lat, d), ctx.reshape(n_batch * ctx_len, d))

    n_route_pad = LANES - N_GROUPS - N_EXPERTS
    for i in range(depth):
        last = i == depth - 1
        j = i // 3
        kind = i % 3
        mods = mods_all[i]
        n_tok = t_lat if last else t_all
        g1, b1 = ln1_g[i].reshape(1, d), ln1_b[i].reshape(1, d)
        g2, b2 = ln2_g[i].reshape(1, d), ln2_b[i].reshape(1, d)
        wr = jnp.concatenate([router_w_group[i], router_w_expert[i], jnp.zeros((d, n_route_pad), F32)], axis=1).T
        wr_top = _bf16_part(wr)
        wr = jnp.concatenate([wr_top.astype(BF16), (wr - wr_top).astype(BF16)], axis=0)
        br = jnp.concatenate([router_b_group[i], router_b_expert[i], jnp.zeros((n_route_pad,), F32)]).reshape(LANES, 1)

        if kind == 0:
            q, k, v = _proj_in(0, xs, mods, attn_w_qkv[j].astype(BF16), dims, rope_tabs)
            o = _window_attn(q, k, v, attn_sink[j], dims)
            o_ctx = o if last else _ctx_attn(q, k, v, attn_sink[j], dims, A_KV_HEADS, True)
            post = _post_mixer_call(0, (o, o_ctx), xs, mods, g1, b1, attn_w_o[j].astype(BF16), wr, br, dims, n_tok, alpha)
        elif kind == 1:
            gb, z = _proj_in(1, xs, mods, conv_w_in[j].astype(BF16), dims)
            post = _post_mixer_call(1, (gb, z), xs, mods, g1, b1, conv_w_out[j].astype(BF16), wr, br, dims, n_tok, alpha,
                                    conv_w=conv_w[j])
        else:
            q, k, v = _proj_in(2, xs, mods, nat_w_qkv[j].astype(BF16), dims)
            o = _nat_attn(q, k, v, nat_rpb[j], dims)
            o_ctx = o if last else _ctx_attn(q, k, v, jnp.zeros((N_HEADS,), F32), dims, N_HEADS, False)
            post = _post_mixer_call(2, (o, o_ctx), xs, mods, g1, b1, nat_w_o[j].astype(BF16), wr, br, dims, n_tok, alpha)

        x1, xs_c, route, meta = post
        chunk_src, blk_e, n_used, n_blocks = _expert_schedule(meta, rc, EXPERT_BLOCK)
        ys_c = _experts(xs_c, chunk_src, blk_e, n_used, n_blocks, expert_w_gate, expert_w_up, expert_w_down, i, rc,
                        EXPERT_BLOCK)
        xs = _combine(ys_c, x1, route, mods, g2, b2, dims, n_tok, alpha, rc)
    return xs[:t_lat].reshape(n_batch, seq, d)
```

```python
import functools

import jax
import jax.numpy as jnp
from jax import lax
from jax.experimental import pallas as pl
from jax.experimental.pallas import tpu as pltpu

F32 = jnp.float32
BF16 = jnp.bfloat16
HIGHEST = lax.Precision.HIGHEST

D_MODEL = 1024
HEAD_DIM = 64
N_HEADS = D_MODEL // HEAD_DIM
A_KV_HEADS = N_HEADS // 4
A_KV = A_KV_HEADS * HEAD_DIM
A_WINDOW = 128
A_BLOCK = 128
GRID_W = 64
ROPE_BASE = 10000.0
NA_KH_MAX = 8
NA_KW = 16
N_GROUPS = 4
EXPERTS_PER_GROUP = 8
N_EXPERTS = N_GROUPS * EXPERTS_PER_GROUP
TOP_K = 2
EXPERT_FF = D_MODEL // 2
LN_EPS = 1e-5
NEG_INF = -1e30

LANES = 128
ADA_ROWS = 24
ROUTE_W = 8
TOKEN_TILE = 512
MOE_TILE = 256
EXPERT_BLOCK = 512
EXPERT_PARTS = 2
CHUNK = 8
TRASH_ROWS = 32
ROUTE_ROWS = 64
NAT_ROWS = 4
NAT_WINDOW_ROWS = 12
VMEM_LIMIT = 56 * 1024 * 1024


def _cparams(n_axes=1):
    return pltpu.CompilerParams(dimension_semantics=("arbitrary",) * n_axes, vmem_limit_bytes=VMEM_LIMIT)


def _ada_kernel(c_ref, w_ref, b_ref, o_ref):
    cc = c_ref[...]
    s = cc / (1.0 + jnp.exp(-cc))
    o_ref[...] = jnp.dot(s, w_ref[...], precision=HIGHEST, preferred_element_type=F32) + b_ref[...]


def _ada(cc, w_ada, b_ada):
    depth, d, n = w_ada.shape
    nt = n // 4
    return pl.pallas_call(
        _ada_kernel,
        out_shape=jax.ShapeDtypeStruct((depth, ADA_ROWS, n), F32),
        grid=(depth, n // nt),
        in_specs=[
            pl.BlockSpec((ADA_ROWS, d), lambda i, j: (0, 0)),
            pl.BlockSpec((None, d, nt), lambda i, j: (i, 0, j)),
            pl.BlockSpec((None, 1, nt), lambda i, j: (i, 0, j)),
        ],
        out_specs=pl.BlockSpec((None, ADA_ROWS, nt), lambda i, j: (i, 0, j)),
        compiler_params=_cparams(2),
        name="ada_mod",
    )(cc, w_ada, b_ada.reshape(depth, 1, n))


def _modulated(x_ref, mod_ref, shift_row):
    return x_ref[...] * (1.0 + mod_ref[shift_row + 1:shift_row + 2, :]) + mod_ref[shift_row:shift_row + 1, :]


def _layer_norm(r, g_ref, b_ref):
    mu = jnp.mean(r, axis=-1, keepdims=True)
    rc = r - mu
    var = jnp.mean(rc * rc, axis=-1, keepdims=True)
    return rc * lax.rsqrt(var + LN_EPS) * g_ref[...] + b_ref[...]


def _dot_nt(a, b):
    return lax.dot_general(a, b, (((1,), (1,)), ((), ())), preferred_element_type=F32)


def _softmax_pv(s, v, sink_col=None):
    m = jnp.max(s, axis=-1, keepdims=True)
    if sink_col is not None:
        m = jnp.maximum(m, sink_col)
    e = jnp.exp(s - m)
    den = jnp.sum(e, axis=-1, keepdims=True)
    if sink_col is not None:
        den = den + jnp.exp(sink_col - m)
    return jnp.dot(e.astype(BF16), v, preferred_element_type=F32) / den


def _softmax_pv_ones(s, v_ones, nv):
    m = jnp.max(s, axis=-1, keepdims=True)
    both = jnp.dot(jnp.exp(s - m).astype(BF16), v_ones, preferred_element_type=F32)
    return both[:, :nv] / both[:, nv:nv + 1]


def _seg_of_tile(i, n_lat_tiles, tiles_per_batch, n_batch):
    return jnp.where(i < n_lat_tiles, i // tiles_per_batch, n_batch)


class _TileOfTwo:
    def __init__(self, lat_ref, ctx_ref, is_lat):
        self.lat_ref, self.ctx_ref, self.is_lat = lat_ref, ctx_ref, is_lat

    def __getitem__(self, idx):
        return jnp.where(self.is_lat, self.lat_ref[idx], self.ctx_ref[idx])


def _stream_specs(xs, tm, tile):
    if not isinstance(xs, tuple):
        return (xs,), [pl.BlockSpec((tm, xs.shape[1]), lambda i: (tile(i), 0))]
    n_lat, n_ctx = xs[0].shape[0] // tm, xs[1].shape[0] // tm
    return xs, [pl.BlockSpec((tm, xs[0].shape[1]), lambda i: (jnp.minimum(tile(i), n_lat - 1), 0)),
                pl.BlockSpec((tm, xs[1].shape[1]), lambda i: (jnp.clip(tile(i) - n_lat, 0, n_ctx - 1), 0))]


def _with_stream(body, pos, xs, tm, tile):
    if not isinstance(xs, tuple):
        return body
    n_lat = xs[0].shape[0] // tm

    def wrapped(*refs):
        x = _TileOfTwo(refs[pos], refs[pos + 1], tile(pl.program_id(0)) < n_lat)
        return body(*refs[:pos], x, *refs[pos + 2:])
    return wrapped


def _proj_attn_kernel(x_ref, mod_ref, cos_ref, sin_ref, w_ref, q_ref, k_ref, v_ref):
    h = _modulated(x_ref, mod_ref, 0).astype(BF16)
    cos = cos_ref[...]
    sin = sin_ref[...]
    lane = lax.broadcasted_iota(jnp.int32, cos.shape, 1)
    low_half = (lane % 32) < 16

    def rope(a):
        partner = jnp.where(low_half, pltpu.roll(a, LANES - 16, 1), pltpu.roll(a, 16, 1))
        return a * cos + partner * sin

    def rope_cols(c0, width):
        a = jnp.dot(h, w_ref[:, c0:c0 + width], preferred_element_type=F32)
        return jnp.concatenate([rope(a[:, j * LANES:(j + 1) * LANES]) for j in range(width // LANES)], axis=1)

    nq = q_ref.shape[1]
    nk = k_ref.shape[1]
    wide = 2 * LANES
    for j in range(nq // wide):
        q_ref[:, j * wide:(j + 1) * wide] = (rope_cols(j * wide, wide) * (HEAD_DIM ** -0.5)).astype(BF16)
    for j in range(nk // wide):
        k_ref[:, j * wide:(j + 1) * wide] = rope_cols(nq + j * wide, wide).astype(BF16)
    v_ref[...] = jnp.dot(h, w_ref[:, nq + nk:], preferred_element_type=F32).astype(BF16)


def _proj_nat_kernel(x_ref, mod_ref, w_ref, q_ref, k_ref, v_ref):
    h = _modulated(x_ref, mod_ref, 0).astype(BF16)
    d = q_ref.shape[1]
    q_ref[...] = (jnp.dot(h, w_ref[:, :d], preferred_element_type=F32) * (HEAD_DIM ** -0.5)).astype(BF16)
    k_ref[...] = jnp.dot(h, w_ref[:, d:2 * d], preferred_element_type=F32).astype(BF16)
    v_ref[...] = jnp.dot(h, w_ref[:, 2 * d:], preferred_element_type=F32).astype(BF16)


def _proj_conv_kernel(x_ref, mod_ref, w_ref, gb_ref, z_ref):
    h = _modulated(x_ref, mod_ref, 0).astype(BF16)
    d = gb_ref.shape[1]
    gb_ref[...] = jnp.dot(h, w_ref[:, :d], preferred_element_type=F32).astype(BF16)
    gc = jnp.dot(h, w_ref[:, d:2 * d], preferred_element_type=F32)
    u = jnp.dot(h, w_ref[:, 2 * d:], preferred_element_type=F32)
    z_ref[...] = (gc * u).astype(BF16)


def _proj_in(kind, xs, mods, w_bf, dims, rope_tabs=None):
    n_batch, seq, ctx_len = dims
    t = sum(a.shape[0] for a in xs) if isinstance(xs, tuple) else xs.shape[0]
    d = w_bf.shape[0]
    tm = TOKEN_TILE
    n_lat_tiles = n_batch * seq // tm
    tpb = seq // tm
    seg = functools.partial(_seg_of_tile, n_lat_tiles=n_lat_tiles, tiles_per_batch=tpb, n_batch=n_batch)
    x_ins, x_specs = _stream_specs(xs, tm, lambda i: i)
    mod_spec = pl.BlockSpec((None, 6, d), lambda i: (seg(i), 0, 0))
    w_spec = pl.BlockSpec(w_bf.shape, lambda i: (0, 0))

    def tok_out(n):
        return jax.ShapeDtypeStruct((t, n), BF16), pl.BlockSpec((tm, n), lambda i: (i, 0))

    if kind == 0:
        cos, sin = rope_tabs
        tab_spec = pl.BlockSpec((tm, LANES), lambda i: (jnp.where(i < n_lat_tiles, i % tpb, tpb), 0))
        outs = [tok_out(d), tok_out(A_KV), tok_out(A_KV)]
        body, ins, in_specs = _proj_attn_kernel, (mods, cos, sin, w_bf), [mod_spec, tab_spec, tab_spec, w_spec]
    elif kind == 1:
        outs = [tok_out(d), tok_out(d)]
        body, ins, in_specs = _proj_conv_kernel, (mods, w_bf), [mod_spec, w_spec]
    else:
        outs = [tok_out(d), tok_out(d), tok_out(d)]
        body, ins, in_specs = _proj_nat_kernel, (mods, w_bf), [mod_spec, w_spec]
    return pl.pallas_call(
        _with_stream(body, 0, xs, tm, lambda i: i),
        out_shape=[o[0] for o in outs],
        grid=(t // tm,),
        in_specs=x_specs + in_specs,
        out_specs=[o[1] for o in outs],
        compiler_params=_cparams(1),
        name=f"proj_in_{kind}",
    )(*x_ins, *ins)


def _rope_tables(seq, tm):
    quarter = HEAD_DIM // 4
    inv_freq = ROPE_BASE ** (-jnp.arange(quarter, dtype=F32) / quarter)
    tpos = jnp.arange(seq)
    ang_r = (tpos // GRID_W).astype(F32)[:, None] * inv_freq
    ang_c = (tpos % GRID_W).astype(F32)[:, None] * inv_freq
    cos_h = jnp.concatenate([jnp.cos(ang_r), jnp.cos(ang_r), jnp.cos(ang_c), jnp.cos(ang_c)], axis=1)
    sin_h = jnp.concatenate([-jnp.sin(ang_r), jnp.sin(ang_r), -jnp.sin(ang_c), jnp.sin(ang_c)], axis=1)
    reps = LANES // HEAD_DIM
    cos = jnp.concatenate([jnp.tile(cos_h, (1, reps)), jnp.ones((tm, LANES), F32)], axis=0)
    sin = jnp.concatenate([jnp.tile(sin_h, (1, reps)), jnp.zeros((tm, LANES), F32)], axis=0)
    return cos, sin


def _window_attn_kernel(sink_ref, q_ref, kp_ref, kc_ref, kn_ref, kx_ref, vp_ref, vc_ref, vn_ref, vx_ref, o_ref, *, seq):
    qi = pl.program_id(1)
    blk = A_BLOCK
    group = N_HEADS // A_KV_HEADS
    kcat = jnp.concatenate([kp_ref[...], kc_ref[...], kn_ref[...], kx_ref[...]], axis=0)
    vcat = jnp.concatenate([vp_ref[...], vc_ref[...], vn_ref[...], vx_ref[...]], axis=0)
    rows = group * blk
    row = lax.broadcasted_iota(jnp.int32, (rows, blk), 0) & (blk - 1)
    col = lax.broadcasted_iota(jnp.int32, (rows, blk), 1)
    prev_ok = (col + (blk - A_WINDOW) >= row) & (qi > 0)
    next_ok = (col - (blk - A_WINDOW) <= row) & (qi < seq // blk - 1)
    for kk in range(A_KV_HEADS):
        k_h = kcat[:, kk * HEAD_DIM:(kk + 1) * HEAD_DIM]
        v_h = vcat[:, kk * HEAD_DIM:(kk + 1) * HEAD_DIM]
        heads = [kk * group + g for g in range(group)]
        q_st = jnp.concatenate([q_ref[:, h * HEAD_DIM:(h + 1) * HEAD_DIM] for h in heads], axis=0)
        sink_col = jnp.concatenate([jnp.full((blk, 1), sink_ref[h], F32) for h in heads], axis=0)
        s = _dot_nt(q_st, k_h)
        s = jnp.concatenate([jnp.where(prev_ok, s[:, :blk], NEG_INF), s[:, blk:2 * blk],
                             jnp.where(next_ok, s[:, 2 * blk:3 * blk], NEG_INF), s[:, 3 * blk:]], axis=1)
        o = _softmax_pv(s, v_h, sink_col).astype(o_ref.dtype)
        for g, h in enumerate(heads):
            o_ref[:, h * HEAD_DIM:(h + 1) * HEAD_DIM] = o[g * blk:(g + 1) * blk, :]


def _window_attn(q, k, v, sink, dims):
    n_batch, seq, ctx_len = dims
    t, d = q.shape
    blk = A_BLOCK
    nb = seq // blk
    t_lat = n_batch * seq
    ctx_blk0 = t_lat // ctx_len

    def q_map(b, i, s):
        return (b * nb + i, 0)

    def kv_map(off):
        return lambda b, i, s: (b * nb + jnp.clip(i + off, 0, nb - 1), 0)

    def ctx_map(b, i, s):
        return (ctx_blk0 + b, 0)

    kv_specs = [pl.BlockSpec((blk, A_KV), kv_map(-1)), pl.BlockSpec((blk, A_KV), kv_map(0)),
                pl.BlockSpec((blk, A_KV), kv_map(1)), pl.BlockSpec((ctx_len, A_KV), ctx_map)]
    return pl.pallas_call(
        functools.partial(_window_attn_kernel, seq=seq),
        out_shape=jax.ShapeDtypeStruct((t_lat, d), BF16),
        grid_spec=pltpu.PrefetchScalarGridSpec(
            num_scalar_prefetch=1,
            grid=(n_batch, nb),
            in_specs=[pl.BlockSpec((blk, d), q_map)] + kv_specs + kv_specs,
            out_specs=pl.BlockSpec((blk, d), q_map),
        ),
        compiler_params=_cparams(2),
        name="window_attn",
    )(sink, q, k, k, k, k, v, v, v, v)


def _ctx_attn_kernel(sink_ref, q_ref, k_ref, v_ref, o_ref, *, n_kv, use_sink):
    group = N_HEADS // n_kv
    n = q_ref.shape[0]
    for kk in range(n_kv):
        k_h = k_ref[:, kk * HEAD_DIM:(kk + 1) * HEAD_DIM]
        v_h = v_ref[:, kk * HEAD_DIM:(kk + 1) * HEAD_DIM]
        heads = [kk * group + g for g in range(group)]
        q_st = jnp.concatenate([q_ref[:, h * HEAD_DIM:(h + 1) * HEAD_DIM] for h in heads], axis=0)
        sink_col = None
        if use_sink:
            sink_col = jnp.concatenate([jnp.full((n, 1), sink_ref[h], F32) for h in heads], axis=0)
        o = _softmax_pv(_dot_nt(q_st, k_h), v_h, sink_col).astype(o_ref.dtype)
        for g, h in enumerate(heads):
            o_ref[:, h * HEAD_DIM:(h + 1) * HEAD_DIM] = o[g * n:(g + 1) * n, :]


def _ctx_attn(q, k, v, sink, dims, n_kv, use_sink):
    n_batch, seq, ctx_len = dims
    d = q.shape[1]
    kvw = n_kv * HEAD_DIM
    blk0 = n_batch * seq // ctx_len
    return pl.pallas_call(
        functools.partial(_ctx_attn_kernel, n_kv=n_kv, use_sink=use_sink),
        out_shape=jax.ShapeDtypeStruct((n_batch * ctx_len, d), BF16),
        grid_spec=pltpu.PrefetchScalarGridSpec(
            num_scalar_prefetch=1,
            grid=(n_batch,),
            in_specs=[pl.BlockSpec((ctx_len, d), lambda b, s: (blk0 + b, 0)),
                      pl.BlockSpec((ctx_len, kvw), lambda b, s: (blk0 + b, 0)),
                      pl.BlockSpec((ctx_len, kvw), lambda b, s: (blk0 + b, 0))],
            out_specs=pl.BlockSpec((ctx_len, d), lambda b, s: (b, 0)),
        ),
        compiler_params=_cparams(1),
        name="ctx_attn",
    )(sink, q, k, v)


def _nat_kernel(q_ref, k0_ref, k1_ref, k2_ref, kx_ref, v0_ref, v1_ref, v2_ref, vx_ref, bias_ref, o_ref):
    nq = q_ref.shape[0]
    low = lax.broadcasted_iota(jnp.int32, (nq, LANES), 1) < HEAD_DIM
    for a in range(N_HEADS // 2):
        sl = slice(a * LANES, (a + 1) * LANES)
        qp = q_ref[:, sl]
        k_loc = jnp.concatenate([k0_ref[:, sl], k1_ref[:, sl], k2_ref[:, sl]], axis=0)
        k_ctx = kx_ref[:, sl]
        v_all = jnp.concatenate([v0_ref[:, sl], v1_ref[:, sl], v2_ref[:, sl], vx_ref[:, sl]], axis=0)
        v_all = jnp.concatenate([v_all, jnp.ones_like(v_all)], axis=1)
        outs = []
        for hh in range(2):
            qm = jnp.where(low if hh == 0 else jnp.logical_not(low), qp, jnp.zeros_like(qp))
            s = jnp.concatenate([_dot_nt(qm, k_loc) + bias_ref[2 * a + hh], _dot_nt(qm, k_ctx)], axis=-1)
            outs.append(_softmax_pv_ones(s, v_all, LANES))
        o_ref[:, sl] = jnp.where(low, outs[0], outs[1]).astype(o_ref.dtype)


def _nat_bias_table(rpb, kh):
    g, wr = NAT_ROWS, NAT_WINDOW_ROWS
    n_dr, n_dc = 2 * NA_KH_MAX - 1, 2 * NA_KW - 1
    table = jnp.full((N_HEADS, n_dr + 1, n_dc + 1), NEG_INF, F32).at[:, :n_dr, :n_dc].set(rpb.astype(F32))
    qc = jnp.arange(GRID_W)
    kc = jnp.arange(GRID_W)
    q_start = jnp.clip(qc - NA_KW // 2, 0, GRID_W - NA_KW)
    col_ok = (kc[None, :] >= q_start[:, None]) & (kc[None, :] < q_start[:, None] + NA_KW)
    dc = jnp.where(col_ok, jnp.clip(kc[None, :] - qc[:, None] + NA_KW - 1, 0, n_dc - 1), n_dc)
    dc_hot = (dc[:, :, None] == jnp.arange(n_dc + 1)).astype(F32)
    by_col = jnp.einsum('hac,qkc->haqk', table, dc_hot, precision=HIGHEST)
    j = jnp.arange(g)
    i = jnp.arange(wr)
    tables = []
    for q_rel, rs_rel in ((j, 0 * j), (g + j, j), (wr - g + j, 0 * j + wr - kh)):
        row_ok = (i[None, :] >= rs_rel[:, None]) & (i[None, :] < rs_rel[:, None] + kh)
        dr = jnp.where(row_ok, jnp.clip(i[None, :] - q_rel[:, None] + NA_KH_MAX - 1, 0, n_dr - 1), n_dr)
        dr_hot = (dr[:, :, None] == jnp.arange(n_dr + 1)).astype(F32)
        b = jnp.einsum('jia,haqk->hjqik', dr_hot, by_col, precision=HIGHEST)
        tables.append(b.reshape(N_HEADS, g * GRID_W, wr * GRID_W))
    return jnp.stack(tables)


def _nat_attn(q, k, v, rpb, dims):
    n_batch, seq, ctx_len = dims
    t, d = q.shape
    n_rows = seq // GRID_W
    kh = min(NA_KH_MAX, n_rows)
    g, wr = NAT_ROWS, NAT_WINDOW_ROWS
    n_groups = n_rows // g
    assert kh == NA_KH_MAX and wr == 3 * g and wr >= kh + g - 1 and n_rows % g == 0 and n_groups >= 3
    bias = _nat_bias_table(rpb, kh)
    blk = g * GRID_W
    bpb = seq // blk
    ctx_blk0 = n_batch * seq // ctx_len

    def case_of(m):
        return jnp.where(m == 0, 0, jnp.where(m == n_groups - 1, 2, 1))

    def win(off):
        return lambda m, b: (b * bpb + jnp.clip(m - 1, 0, n_groups - 3) + off, 0)

    tok = pl.BlockSpec((blk, d), lambda m, b: (b * bpb + m, 0))
    ctx = pl.BlockSpec((ctx_len, d), lambda m, b: (ctx_blk0 + b, 0))
    kv = [pl.BlockSpec((blk, d), win(0)), pl.BlockSpec((blk, d), win(1)), pl.BlockSpec((blk, d), win(2)), ctx]
    return pl.pallas_call(
        _nat_kernel,
        out_shape=jax.ShapeDtypeStruct((n_batch * seq, d), BF16),
        grid=(n_groups, n_batch),
        in_specs=[tok] + kv + kv + [
            pl.BlockSpec((None, N_HEADS, blk, wr * GRID_W), lambda m, b: (case_of(m), 0, 0, 0))],
        out_specs=tok,
        compiler_params=_cparams(2),
        name="nat_attn",
    )(q, k, k, k, k, v, v, v, v, bias)


def _bf16_part(a):
    return lax.bitcast_convert_type(lax.bitcast_convert_type(a, jnp.uint32) & jnp.uint32(0xFFFF0000), F32)


def _route_tile(h2, wr_ref, br_ref, rc):
    mt = h2.shape[0]
    h_top = _bf16_part(h2)
    both = _dot_nt(wr_ref[...], h_top.astype(BF16)) + _dot_nt(wr_ref[...], (h2 - h_top).astype(BF16))
    logits = (both[:LANES] + both[LANES:] + br_ref[...])[:ROUTE_ROWS]
    row_i = lax.broadcasted_iota(jnp.int32, logits.shape, 0)
    row = row_i.astype(F32)
    big = jnp.float32(1e9)
    low = jnp.float32(-3e38)
    is_group = row_i < N_GROUPS
    gl = jnp.where(is_group, logits, low)
    gmax = jnp.max(gl, axis=0, keepdims=True)
    g_idx = jnp.min(jnp.where(gl == gmax, row, big), axis=0, keepdims=True)
    gsum = jnp.sum(jnp.where(is_group, jnp.exp(gl - gmax), 0.0), axis=0, keepdims=True)
    g_prob = 1.0 / gsum
    e_lo = N_GROUPS + g_idx * EXPERTS_PER_GROUP
    in_group = (row >= e_lo) & (row < e_lo + EXPERTS_PER_GROUP)
    el = jnp.where(in_group, logits, low)
    v1 = jnp.max(el, axis=0, keepdims=True)
    i1 = jnp.min(jnp.where(el == v1, row, big), axis=0, keepdims=True)
    el2 = jnp.where(row == i1, low, el)
    v2 = jnp.max(el2, axis=0, keepdims=True)
    i2 = jnp.min(jnp.where(el2 == v2, row, big), axis=0, keepdims=True)
    t = jnp.exp(v2 - v1)
    g0 = 1.0 / (1.0 + t) * g_prob
    g1 = t / (1.0 + t) * g_prob

    hit0 = row == i1
    hit1 = row == i2
    onehot = jnp.where(hit0, 1.0, jnp.where(hit1, 1.0, 0.0))
    r_i = lax.broadcasted_iota(jnp.int32, (mt, mt), 0)
    c_i = lax.broadcasted_iota(jnp.int32, (mt, mt), 1)
    earlier = jnp.where(r_i < c_i, 1.0, 0.0).astype(BF16)
    before = jnp.dot(onehot.astype(BF16), earlier, preferred_element_type=F32)
    cnt = jnp.sum(onehot, axis=1, keepdims=True)
    padded = jnp.floor((cnt + (CHUNK - 1)) * (1.0 / CHUNK)) * CHUNK
    a_i = lax.broadcasted_iota(jnp.int32, (ROUTE_ROWS, ROUTE_ROWS), 0)
    b_i = lax.broadcasted_iota(jnp.int32, (ROUTE_ROWS, ROUTE_ROWS), 1)
    lower = jnp.where(b_i < a_i, 1.0, 0.0)
    goff = jnp.dot(lower, jnp.broadcast_to(padded, (ROUTE_ROWS, LANES)), precision=HIGHEST,
                   preferred_element_type=F32)[:, 0:1]
    pos = before + goff
    slot0 = jnp.sum(jnp.where(hit0, pos, 0.0), axis=0, keepdims=True)
    slot1 = jnp.sum(jnp.where(hit1, pos, 0.0), axis=0, keepdims=True)
    return (i1 - N_GROUPS, i2 - N_GROUPS, g0, g1, slot0, slot1), cnt, goff


def _to_slabs(a):
    rows, d = a.shape
    g = a.reshape(rows // CHUNK, CHUNK, d)
    return jnp.concatenate([g[:, :, :d // 2], g[:, :, d // 2:]], axis=1).astype(BF16)


def _from_slabs(s):
    n, _, half = s.shape
    g = s.astype(F32)
    return (g[:, :CHUNK, :].reshape(n * CHUNK, half).astype(BF16), g[:, CHUNK:, :].reshape(n * CHUNK, half).astype(BF16))


def _post_mixer(o_bf, x_ref, mod_ref, g_ref, b_ref, wo_ref, wr_ref, br_ref, x1_ref, xs_ref, route_ref, meta_ref, alpha):
    y = jnp.dot(o_bf, wo_ref[...], preferred_element_type=F32)
    x1 = _layer_norm(alpha * x_ref[...] + mod_ref[2:3, :] * y, g_ref, b_ref)
    x1_ref[...] = x1
    h2 = x1 * (1.0 + mod_ref[4:5, :]) + mod_ref[3:4, :]
    rc = xs_ref.shape[0] * CHUNK
    mt = h2.shape[0]
    rows, cnt, goff = _route_tile(h2, wr_ref, br_ref, rc)

    s_i = lax.broadcasted_iota(jnp.int32, (rc, mt), 0).astype(F32)
    pick = jnp.where(s_i == rows[4], 1.0, jnp.where(s_i == rows[5], 1.0, 0.0)).astype(BF16)
    xs_ref[...] = _to_slabs(jnp.dot(pick, h2.astype(BF16), preferred_element_type=F32))

    sub = lax.broadcasted_iota(jnp.int32, (LANES, mt), 0)
    stacked = jnp.zeros((LANES, mt), F32)
    for j, r in enumerate(rows):
        stacked = jnp.where(sub == j, r, stacked)
    route_ref[...] = stacked.T[:, :ROUTE_W]
    lane = lax.broadcasted_iota(jnp.int32, (ROUTE_ROWS, LANES), 1)
    meta_ref[...] = jnp.where(lane == 0, cnt, jnp.where(lane == 1, goff, 0.0))


def _post_attn_kernel(ol_ref, oc_ref, x_ref, mod_ref, g_ref, b_ref, wo_ref, wr_ref, br_ref, x1_ref, xs_ref, route_ref, meta_ref,
                      *, alpha, n_lat_tiles, n_tiles):
    is_lat = jnp.minimum(pl.program_id(0), n_tiles - 1) < n_lat_tiles
    o = jnp.where(is_lat, ol_ref[...], oc_ref[...])
    _post_mixer(o, x_ref, mod_ref, g_ref, b_ref, wo_ref, wr_ref, br_ref, x1_ref, xs_ref, route_ref, meta_ref, alpha)


def _post_conv_kernel(gb_ref, z_ref, zp_ref, zn_ref, cw_ref, x_ref, mod_ref, g_ref, b_ref, wo_ref, wr_ref, br_ref,
                      x1_ref, xs_ref, route_ref, meta_ref, *, alpha, t_lat, seq, ctx_len, n_tiles):
    i = jnp.minimum(pl.program_id(0), n_tiles - 1)
    tm, d = z_ref.shape
    halo = zp_ref.shape[0]
    z = z_ref[...].astype(F32)
    row = lax.broadcasted_iota(jnp.int32, (tm, 1), 0)
    g = i * tm + row
    pos = jnp.where(g < t_lat, g % seq, g % ctx_len)
    n_seq = jnp.where(g < t_lat, seq, ctx_len)
    prev_row = zp_ref[halo - 1:halo, :].astype(F32)
    next_row = zn_ref[0:1, :].astype(F32)
    z_prev = jnp.where(row == 0, prev_row, pltpu.roll(z, 1, 0))
    z_next = jnp.where(row == tm - 1, next_row, pltpu.roll(z, tm - 1, 0))
    z_prev = jnp.where(pos == 0, 0.0, z_prev)
    z_next = jnp.where(pos == n_seq - 1, 0.0, z_next)
    y = cw_ref[0:1, :] * z_prev + cw_ref[1:2, :] * z + cw_ref[2:3, :] * z_next
    o = (gb_ref[...].astype(F32) * y).astype(BF16)
    _post_mixer(o, x_ref, mod_ref, g_ref, b_ref, wo_ref, wr_ref, br_ref, x1_ref, xs_ref, route_ref, meta_ref, alpha)


def _compact_rows(mt):
    return TOP_K * mt + N_EXPERTS * CHUNK


def _spare_tiles(rc, bm):
    return -(-2 * bm // rc)


def _post_mixer_call(kind, mixer_out, xs, mods, ln_g, ln_b, wo_bf, wr, br, dims, n_tok, alpha, conv_w=None):
    n_batch, seq, ctx_len = dims
    d = wo_bf.shape[1]
    tm = MOE_TILE
    rc = _compact_rows(tm)
    n_tiles = n_tok // tm
    n_lat_tiles = n_batch * seq // tm
    tpb = seq // tm
    seg = functools.partial(_seg_of_tile, n_lat_tiles=n_lat_tiles, tiles_per_batch=tpb, n_batch=n_batch)
    n_spare = _spare_tiles(rc, EXPERT_BLOCK)
    tile = lambda i: jnp.minimum(i, n_tiles - 1)
    tok = pl.BlockSpec((tm, d), lambda i: (tile(i), 0))
    full = lambda a: pl.BlockSpec(a.shape, lambda i: (0,) * a.ndim)
    x_ins, x_specs = _stream_specs(xs, tm, tile)
    common_ins = x_ins + (mods, ln_g, ln_b, wo_bf, wr, br)
    common_specs = x_specs + [pl.BlockSpec((None, 6, d), lambda i: (seg(tile(i)), 0, 0)), full(ln_g), full(ln_b),
                              full(wo_bf), full(wr), full(br)]
    if kind == 1:
        gb, z = mixer_out
        halo = 16
        hb = tm // halo
        last_blk = z.shape[0] // halo - 1
        ins = (gb, z, z, z, conv_w) + common_ins
        in_specs = [tok, tok,
                    pl.BlockSpec((halo, d), lambda i: (jnp.maximum(tile(i) * hb - 1, 0), 0)),
                    pl.BlockSpec((halo, d), lambda i: (jnp.minimum((tile(i) + 1) * hb, last_blk), 0)),
                    full(conv_w)] + common_specs
        body = _with_stream(functools.partial(_post_conv_kernel, alpha=alpha, t_lat=n_batch * seq, seq=seq,
                                              ctx_len=ctx_len, n_tiles=n_tiles), 5, xs, tm, tile)
    else:
        o_lat, o_ctx = mixer_out
        n_ctx_tiles = o_ctx.shape[0] // tm
        ins = (o_lat, o_ctx) + common_ins
        in_specs = [pl.BlockSpec((tm, d), lambda i: (jnp.minimum(tile(i), n_lat_tiles - 1), 0)),
                    pl.BlockSpec((tm, d), lambda i: (jnp.clip(tile(i) - n_lat_tiles, 0, n_ctx_tiles - 1), 0))] + common_specs
        body = _with_stream(functools.partial(_post_attn_kernel, alpha=alpha, n_lat_tiles=n_lat_tiles, n_tiles=n_tiles),
                            2, xs, tm, tile)
    return pl.pallas_call(
        body,
        out_shape=[jax.ShapeDtypeStruct((n_tok, d), F32),
                   jax.ShapeDtypeStruct(((n_tiles + n_spare) * rc // CHUNK, 2 * CHUNK, d // 2), BF16),
                   jax.ShapeDtypeStruct((n_tok, ROUTE_W), F32),
                   jax.ShapeDtypeStruct((n_tiles, ROUTE_ROWS, LANES), F32)],
        grid=(n_tiles + n_spare,),
        in_specs=in_specs,
        out_specs=[tok, pl.BlockSpec((rc // CHUNK, 2 * CHUNK, d // 2), lambda i: (i, 0, 0)),
                   pl.BlockSpec((tm, ROUTE_W), lambda i: (tile(i), 0)),
                   pl.BlockSpec((None, ROUTE_ROWS, LANES), lambda i: (tile(i), 0, 0))],
        compiler_params=_cparams(1),
        name=f"post_mixer_{kind}",
    )(*ins)


def _expert_schedule(meta, rc, bm):
    n_tiles = meta.shape[0]
    cpb = bm // CHUNK
    cnt = meta[:, N_GROUPS:N_GROUPS + N_EXPERTS, 0].astype(jnp.int32)
    goff = meta[:, N_GROUPS:N_GROUPS + N_EXPERTS, 1].astype(jnp.int32)
    nch = ((cnt + CHUNK - 1) // CHUNK).T
    incl = jnp.cumsum(nch, axis=1)
    per_e = incl[:, -1]
    nbk = (per_e + cpb - 1) // cpb
    blk_start = jnp.cumsum(nbk) - nbk
    src0 = (jnp.arange(n_tiles, dtype=jnp.int32)[None, :] * rc + goff.T) // CHUNK
    n_blocks = (TOP_K * n_tiles * MOE_TILE + (CHUNK - 1) * N_EXPERTS * n_tiles) // CHUNK // cpb + N_EXPERTS
    blk = jnp.arange(n_blocks + 1, dtype=jnp.int32)
    blk_e = jnp.clip(jnp.sum(blk_start[None, :] <= blk[:, None], axis=1) - 1, 0, N_EXPERTS - 1).astype(jnp.int32)
    onehot = (blk_e[:, None] == jnp.arange(N_EXPERTS)[None, :]).astype(F32)
    tables = jnp.concatenate([incl, incl - nch, src0, blk_start[:, None], per_e[:, None]], axis=1).astype(F32)
    per_blk = jnp.round(jnp.dot(onehot, tables, precision=HIGHEST)).astype(jnp.int32)
    incl_b, excl_b, src_b = (per_blk[:, None, i * n_tiles:(i + 1) * n_tiles] for i in range(3))
    start_b, total_b = per_blk[:, 3 * n_tiles], per_blk[:, 3 * n_tiles + 1]
    lq = ((blk - start_b) * cpb)[:, None] + jnp.arange(cpb, dtype=jnp.int32)[None, :]
    run = jnp.sum(incl_b <= lq[:, :, None], axis=2)
    pick = jnp.arange(n_tiles, dtype=jnp.int32)[None, None, :] == run[:, :, None]
    first = jnp.sum(jnp.where(pick, excl_b, 0), axis=2)
    src = jnp.sum(jnp.where(pick, src_b, 0), axis=2)
    real = lq < total_b[:, None]
    piece = src + (lq - first)
    spare = n_tiles * rc // CHUNK + (blk % 2)[:, None] * cpb + jnp.arange(cpb, dtype=jnp.int32)[None, :]
    gather_idx = jnp.where(real, piece, (rc - TRASH_ROWS) // CHUNK).astype(jnp.int32).reshape(-1)
    scatter_idx = jnp.where(real, piece, spare).astype(jnp.int32).reshape(-1)
    n_used = jnp.sum(nbk).astype(jnp.int32).reshape(1)
    return gather_idx, scatter_idx, blk_e, n_used, n_blocks


def _expert_kernel(blk_e_ref, gidx_ref, sidx_ref, n_used_ref, x_hbm, wg_ref, wu_ref, wd_ref, y_hbm, xbuf, ybuf, gsem, ssem,
                   wgu_bf, wd_bf, *, trash_slab):
    b = pl.program_id(0)
    nb = pl.num_programs(0)
    _, cpb, _, half = xbuf.shape
    ff = wg_ref.shape[1]
    n_used = n_used_ref[0]

    def gather(blk, slot):
        for j in range(cpb):
            pltpu.make_async_copy(x_hbm.at[pl.ds(gidx_ref[blk * cpb + j], 1)], xbuf.at[slot, pl.ds(j, 1)],
                                  gsem.at[slot]).start()

    def scatter(blk, slot):
        for j in range(cpb):
            pltpu.make_async_copy(ybuf.at[slot, pl.ds(j, 1)], y_hbm.at[pl.ds(sidx_ref[blk * cpb + j], 1)],
                                  ssem.at[slot]).start()

    def scatter_spare(slot):
        for j in range(cpb):
            pltpu.make_async_copy(ybuf.at[slot, pl.ds(j, 1)], y_hbm.at[pl.ds(trash_slab + slot * cpb + j, 1)],
                                  ssem.at[slot]).start()

    def wait_gather(slot):
        pltpu.make_async_copy(x_hbm.at[pl.ds(0, cpb)], xbuf.at[slot], gsem.at[slot]).wait()

    def wait_scatter(slot):
        pltpu.make_async_copy(ybuf.at[slot], y_hbm.at[pl.ds(0, cpb)], ssem.at[slot]).wait()

    in_use = b < n_used

    @pl.when((b == 0) & in_use)
    def _():
        ybuf[...] = jnp.zeros_like(ybuf)
        scatter_spare(0)
        scatter_spare(1)
        gather(0, 0)

    @pl.when(in_use & ((b == 0) | (blk_e_ref[b] != blk_e_ref[jnp.maximum(b - 1, 0)])))
    def _():
        wgu_bf[:, :ff] = wg_ref[...].astype(BF16)
        wgu_bf[:, ff:] = wu_ref[...].astype(BF16)
        wd_bf[...] = wd_ref[...].astype(BF16)

    @pl.when(in_use)
    def _():
        slot = b % 2
        wait_gather(slot)
        wait_scatter(slot)
        gather(b + 1, 1 - slot)
        for part in range(EXPERT_PARTS):
            slabs = pl.ds(part * (cpb // EXPERT_PARTS), cpb // EXPERT_PARTS)
            xa, xb = _from_slabs(xbuf[slot, slabs])
            gu = (jnp.dot(xa, wgu_bf[:half, :], preferred_element_type=F32)
                  + jnp.dot(xb, wgu_bf[half:, :], preferred_element_type=F32))
            gate = gu[:, :ff]
            act = gate / (1.0 + jnp.exp(-gate)) * gu[:, ff:]
            ybuf[slot, slabs] = _to_slabs(jnp.dot(act.astype(BF16), wd_bf[...], preferred_element_type=F32))
        scatter(b, slot)

    @pl.when((b == nb - 1) & (n_used > 0))
    def _():
        wait_gather(n_used % 2)
        wait_scatter(0)
        wait_scatter(1)


def _experts(xs_c, gather_idx, scatter_idx, blk_e, n_used, n_blocks, w_gate, w_up, w_down, layer, rc, bm):
    d = 2 * xs_c.shape[2]
    ff = w_gate.shape[-1]
    n_tiles = xs_c.shape[0] * CHUNK // rc - _spare_tiles(rc, bm)
    assert rc - TRASH_ROWS >= TOP_K * MOE_TILE + N_EXPERTS * (CHUNK - 1) and bm % (EXPERT_PARTS * CHUNK) == 0
    return pl.pallas_call(
        functools.partial(_expert_kernel, trash_slab=n_tiles * rc // CHUNK),
        out_shape=jax.ShapeDtypeStruct(xs_c.shape, xs_c.dtype),
        grid_spec=pltpu.PrefetchScalarGridSpec(
            num_scalar_prefetch=4,
            grid=(n_blocks,),
            in_specs=[
                pl.BlockSpec(memory_space=pl.ANY),
                pl.BlockSpec((None, None, d, ff), lambda b, be, gi, si, nu: (layer, be[b], 0, 0)),
                pl.BlockSpec((None, None, d, ff), lambda b, be, gi, si, nu: (layer, be[b], 0, 0)),
                pl.BlockSpec((None, None, ff, d), lambda b, be, gi, si, nu: (layer, be[b], 0, 0)),
            ],
            out_specs=pl.BlockSpec(memory_space=pl.ANY),
            scratch_shapes=[
                pltpu.VMEM((2, bm // CHUNK, 2 * CHUNK, d // 2), BF16),
                pltpu.VMEM((2, bm // CHUNK, 2 * CHUNK, d // 2), BF16),
                pltpu.SemaphoreType.DMA((2,)),
                pltpu.SemaphoreType.DMA((2,)),
                pltpu.VMEM((d, 2 * ff), BF16),
                pltpu.VMEM((ff, d), BF16),
            ],
        ),
        input_output_aliases={4: 0},
        compiler_params=_cparams(1),
        name="moe_experts",
    )(blk_e, gather_idx, scatter_idx, n_used, xs_c, w_gate, w_up, w_down)


def _combine_kernel(ys_ref, x1_ref, route_ref, mod_ref, g_ref, b_ref, o_ref, *, alpha):
    rc = ys_ref.shape[0] * CHUNK
    mt = x1_ref.shape[0]
    ya, yb = _from_slabs(ys_ref[...])
    route = route_ref[...]
    s_i = lax.broadcasted_iota(jnp.int32, (mt, rc), 1).astype(F32)
    y = None
    for k in range(TOP_K):
        pick = jnp.where(s_i == route[:, 4 + k:5 + k], 1.0, 0.0).astype(BF16)
        yk = route[:, 2 + k:3 + k] * jnp.concatenate([jnp.dot(pick, ya, preferred_element_type=F32),
                                                      jnp.dot(pick, yb, preferred_element_type=F32)], axis=1)
        y = yk if y is None else y + yk
    o_ref[...] = _layer_norm(alpha * x1_ref[...] + mod_ref[5:6, :] * y, g_ref, b_ref)


def _combine(ys_c, x1, route, mods, ln_g, ln_b, dims, n_tok, alpha, rc):
    n_batch, seq, ctx_len = dims
    d = x1.shape[1]
    tg = MOE_TILE
    n_lat_tiles = n_batch * seq // tg
    tpb = seq // tg
    seg = functools.partial(_seg_of_tile, n_lat_tiles=n_lat_tiles, tiles_per_batch=tpb, n_batch=n_batch)
    return pl.pallas_call(
        functools.partial(_combine_kernel, alpha=alpha),
        out_shape=jax.ShapeDtypeStruct((n_tok, d), F32),
        grid=(n_tok // tg,),
        in_specs=[
            pl.BlockSpec((rc // CHUNK, 2 * CHUNK, d // 2), lambda i: (i, 0, 0)),
            pl.BlockSpec((tg, d), lambda i: (i, 0)),
            pl.BlockSpec((tg, ROUTE_W), lambda i: (i, 0)),
            pl.BlockSpec((None, 6, d), lambda i: (seg(i), 0, 0)),
            pl.BlockSpec(ln_g.shape, lambda i: (0, 0)),
            pl.BlockSpec(ln_b.shape, lambda i: (0, 0)),
        ],
        out_specs=pl.BlockSpec((tg, d), lambda i: (i, 0)),
        compiler_params=_cparams(1),
        name="moe_combine",
    )(ys_c, x1, route, mods, ln_g, ln_b)


def kernel(x, c, ctx, c_ctx, w_ada, b_ada, ln1_g, ln1_b, ln2_g, ln2_b, attn_w_qkv, attn_w_o, attn_sink, conv_w_in, conv_w,
           conv_w_out, nat_w_qkv, nat_w_o, nat_rpb, router_w_group, router_b_group, router_w_expert, router_b_expert,
           expert_w_gate, expert_w_up, expert_w_down):
    n_batch, seq, d = x.shape
    ctx_len = ctx.shape[1]
    depth = w_ada.shape[0]
    dims = (n_batch, seq, ctx_len)
    t_lat = n_batch * seq
    t_all = t_lat + n_batch * ctx_len
    alpha = float((2 * depth) ** 0.25)
    assert d == D_MODEL and n_batch + 1 <= ADA_ROWS
    assert seq % TOKEN_TILE == 0 and (n_batch * ctx_len) % TOKEN_TILE == 0 and seq % GRID_W == 0
    assert TOKEN_TILE % ctx_len == 0 or ctx_len % TOKEN_TILE == 0
    rc = _compact_rows(MOE_TILE)

    cc = jnp.zeros((ADA_ROWS, d), F32).at[:n_batch].set(c).at[n_batch].set(c_ctx)
    mods_all = _ada(cc, w_ada, b_ada).reshape(depth, ADA_ROWS, 6, d)
    rope_tabs = _rope_tables(seq, TOKEN_TILE)
    xs = (x.reshape(t_lat, d), ctx.reshape(n_batch * ctx_len, d))

    n_route_pad = LANES - N_GROUPS - N_EXPERTS
    for i in range(depth):
        last = i == depth - 1
        j = i // 3
        kind = i % 3
        mods = mods_all[i]
        n_tok = t_lat if last else t_all
        g1, b1 = ln1_g[i].reshape(1, d), ln1_b[i].reshape(1, d)
        g2, b2 = ln2_g[i].reshape(1, d), ln2_b[i].reshape(1, d)
        wr = jnp.concatenate([router_w_group[i], router_w_expert[i], jnp.zeros((d, n_route_pad), F32)], axis=1).T
        wr_top = _bf16_part(wr)
        wr = jnp.concatenate([wr_top.astype(BF16), (wr - wr_top).astype(BF16)], axis=0)
        br = jnp.concatenate([router_b_group[i], router_b_expert[i], jnp.zeros((n_route_pad,), F32)]).reshape(LANES, 1)

        if kind == 0:
            q, k, v = _proj_in(0, xs, mods, attn_w_qkv[j].astype(BF16), dims, rope_tabs)
            o = _window_attn(q, k, v, attn_sink[j], dims)
            o_ctx = o if last else _ctx_attn(q, k, v, attn_sink[j], dims, A_KV_HEADS, True)
            post = _post_mixer_call(0, (o, o_ctx), xs, mods, g1, b1, attn_w_o[j].astype(BF16), wr, br, dims, n_tok, alpha)
        elif kind == 1:
            gb, z = _proj_in(1, xs, mods, conv_w_in[j].astype(BF16), dims)
            post = _post_mixer_call(1, (gb, z), xs, mods, g1, b1, conv_w_out[j].astype(BF16), wr, br, dims, n_tok, alpha,
                                    conv_w=conv_w[j])
        else:
            q, k, v = _proj_in(2, xs, mods, nat_w_qkv[j].astype(BF16), dims)
            o = _nat_attn(q, k, v, nat_rpb[j], dims)
            o_ctx = o if last else _ctx_attn(q, k, v, jnp.zeros((N_HEADS,), F32), dims, N_HEADS, False)
            post = _post_mixer_call(2, (o, o_ctx), xs, mods, g1, b1, nat_w_o[j].astype(BF16), wr, br, dims, n_tok, alpha)

        x1, xs_c, route, meta = post
        gather_idx, scatter_idx, blk_e, n_used, n_blocks = _expert_schedule(meta, rc, EXPERT_BLOCK)
        ys_c = _experts(xs_c, gather_idx, scatter_idx, blk_e, n_used, n_blocks, expert_w_gate, expert_w_up, expert_w_down,
                        i, rc, EXPERT_BLOCK)
        xs = _combine(ys_c, x1, route, mods, g2, b2, dims, n_tok, alpha, rc)
    return xs[:t_lat].reshape(n_batch, seq, d)
```
